```python
import math, functools
import jax, jax.numpy as jnp
from jax import lax
import numpy as np

D_MODEL = 4096
BATCH = 4
SEQ = 2048
DEPTH = 2
DEC_BATCH = 8
DEC_SEQ = 4
PAST_LEN = 16384
PAGE_SIZE = 128

N_MIXERS = 2
N_CONV_LAYERS = (DEPTH + 1) // 2
N_ATTN_LAYERS = DEPTH // 2
D_FF = 11008
CONV_W = 3
N_HEADS = 16
HEAD_DIM = 128
V_DIM = 2 * HEAD_DIM
Q_BLOCK = 128
SCALE = HEAD_DIM ** -0.5
RMS_EPS = 1e-6
SUBLN_EPS = 1e-5
NEG_INF = -1e30
N_SUB = 3
N_MOD = 3

kernel_name = "hybrid_shortconv_diffattn_macaron_decoder_step"


def rms_norm(x, g, eps=RMS_EPS):
    x32 = x.astype(jnp.float32)
    y = x32 * lax.rsqrt(jnp.mean(x32 * x32, axis=-1, keepdims=True) + eps)
    return (y * g.astype(jnp.float32)).astype(x.dtype)


def ada_modulation(c, w, b):
    m = jax.nn.silu(c) @ w + b
    return m.reshape(c.shape[0], N_SUB, N_MOD, D_MODEL)


def sandwich(x, mod, g_pre, g_post, fn, coef):
    shift, scale, gate = mod[:, 0, None, :], mod[:, 1, None, :], mod[:, 2, None, :]
    h = rms_norm(x, g_pre) * (1 + scale) + shift
    out, aux = fn(h)
    return x + coef * gate * rms_norm(out, g_post), aux


def swiglu(h, w_in, w_out):
    g, u = jnp.split(h @ w_in, 2, axis=-1)
    return (jax.nn.silu(g) * u) @ w_out, None


def short_conv_mixer(h, prev, w_in, conv_w, w_out):
    T = h.shape[1]
    b_gate, c_gate, xin = jnp.split(h @ w_in, 3, axis=-1)
    u = c_gate * xin
    u_pad = jnp.concatenate([prev.astype(u.dtype), u], axis=1)
    conv = conv_w[0] * u_pad[:, 0:T]
    for k in range(1, CONV_W):
        conv = conv + conv_w[k] * u_pad[:, k:k + T]
    new_state = u_pad[:, u_pad.shape[1] - (CONV_W - 1):]
    return (b_gate * conv) @ w_out, new_state


def diff_lambda(lq1, lk1, lq2, lk2, lam_init):
    f = lambda a: a.astype(jnp.float32)
    return jnp.exp(jnp.sum(f(lq1) * f(lk1))) - jnp.exp(jnp.sum(f(lq2) * f(lk2))) + lam_init


def split_qkv(h, w_qkv):
    B, T, _ = h.shape
    q, k, v = jnp.split(h @ w_qkv, 3, axis=-1)
    return (q.reshape(B, T, N_HEADS, 2, HEAD_DIM),
            k.reshape(B, T, N_HEADS, 2, HEAD_DIM),
            v.reshape(B, T, N_HEADS, V_DIM))


def diff_attn_prompt(q, k, v, lam):
    B, S = q.shape[:2]
    nb = S // Q_BLOCK
    qb = q.reshape(B, nb, Q_BLOCK, N_HEADS, 2, HEAD_DIM).swapaxes(0, 1)
    starts = jnp.arange(nb, dtype=jnp.int32) * Q_BLOCK
    kpos = jnp.arange(S, dtype=jnp.int32)
    v32 = v.astype(jnp.float32)

    def block(args):
        qi, start = args
        s = jnp.einsum('bqhjd,bkhjd->bhjqk', qi, k).astype(jnp.float32) * SCALE
        qpos = start + jnp.arange(Q_BLOCK, dtype=jnp.int32)
        s = jnp.where(kpos[None, :] <= qpos[:, None], s, NEG_INF)
        a = jax.nn.softmax(s, axis=-1)
        d = a[:, :, 0] - lam * a[:, :, 1]
        return jnp.einsum('bhqk,bkhe->bqhe', d, v32)

    o = lax.map(block, (qb, starts))
    return o.swapaxes(0, 1).reshape(B, S, N_HEADS, V_DIM)


def diff_attn_sample(q, k_new, v_new, cache_k, cache_v, layer, page_table, lam):
    T = q.shape[1]
    s = jnp.einsum('bqhjd,bkhjd->bhjqk', q, k_new).astype(jnp.float32) * SCALE
    causal = jnp.tril(jnp.ones((T, T), dtype=bool))
    s = jnp.where(causal, s, NEG_INF)
    m = jnp.max(s, axis=-1)
    p = jnp.exp(s - m[..., None])
    l = jnp.sum(p, axis=-1)
    acc = jnp.einsum('bhjqk,bkhe->bhjqe', p, v_new.astype(jnp.float32))

    def step(carry, pages):
        m, l, acc = carry
        kp = cache_k[layer, pages]
        vp = cache_v[layer, pages]
        s = jnp.einsum('bqhjd,bkhjd->bhjqk', q, kp).astype(jnp.float32) * SCALE
        m_new = jnp.maximum(m, jnp.max(s, axis=-1))
        corr = jnp.exp(m - m_new)
        p = jnp.exp(s - m_new[..., None])
        l = l * corr + jnp.sum(p, axis=-1)
        acc = acc * corr[..., None] + jnp.einsum('bhjqk,bkhe->bhjqe', p, vp.astype(jnp.float32))
        return (m_new, l, acc), None

    (m, l, acc), _ = lax.scan(step, (m, l, acc), page_table.T)
    o = acc / l[..., None]
    d = o[:, :, 0] - lam * o[:, :, 1]
    return d.transpose(0, 2, 1, 3)


def diff_attn_out(o, subln_w, lam_init, w_out):
    B, T = o.shape[:2]
    o = rms_norm(o, subln_w, SUBLN_EPS) * (1 - lam_init)
    return o.reshape(B, T, N_HEADS * V_DIM) @ w_out


def setup_inputs(seed: int = 0) -> dict:
    key = jax.random.key(seed)
    ks = jax.random.split(key, 24)
    f32 = jnp.float32
    n_pages = PAST_LEN // PAGE_SIZE
    n_phys = (DEC_BATCH * n_pages * 5) // 4

    def nrm(k, shape, scale):
        return jax.random.normal(k, shape, f32) * scale

    page_table = jax.random.permutation(ks[5], n_phys)[:DEC_BATCH * n_pages]
    page_table = page_table.reshape(DEC_BATCH, n_pages).astype(jnp.int32)
    return {
        "x_prompt": nrm(ks[0], (BATCH, SEQ, D_MODEL), 1.0),
        "x_sample": nrm(ks[1], (DEC_BATCH, DEC_SEQ, D_MODEL), 1.0),
        "state_conv": nrm(ks[2], (N_CONV_LAYERS, DEC_BATCH, CONV_W - 1, D_MODEL), 1.0),
        "cache_k": nrm(ks[3], (N_ATTN_LAYERS, n_phys, PAGE_SIZE, N_HEADS, 2, HEAD_DIM), 1.0),
        "cache_v": nrm(ks[4], (N_ATTN_LAYERS, n_phys, PAGE_SIZE, N_HEADS, V_DIM), 1.0),
        "page_table": page_table,
        "c_prompt": nrm(ks[6], (BATCH, D_MODEL), 1.0),
        "c_sample": nrm(ks[7], (DEC_BATCH, D_MODEL), 1.0),
        "ada_w": nrm(ks[8], (DEPTH, D_MODEL, N_SUB * N_MOD * D_MODEL), 0.5 * D_MODEL ** -0.5),
        "ada_b": nrm(ks[9], (DEPTH, N_SUB * N_MOD * D_MODEL), 0.02),
        "norm_pre": 1.0 + nrm(ks[10], (DEPTH, N_SUB, D_MODEL), 0.02),
        "norm_post": 1.0 + nrm(ks[11], (DEPTH, N_SUB, D_MODEL), 0.02),
        "ffn_w_in": nrm(ks[12], (DEPTH, 2, D_MODEL, 2 * D_FF), D_MODEL ** -0.5),
        "ffn_w_out": nrm(ks[13], (DEPTH, 2, D_FF, D_MODEL), D_FF ** -0.5),
        "conv_w_in": nrm(ks[14], (N_CONV_LAYERS, D_MODEL, 3 * D_MODEL), D_MODEL ** -0.5),
        "conv_w": nrm(ks[15], (N_CONV_LAYERS, CONV_W, D_MODEL), CONV_W ** -0.5),
        "conv_w_out": nrm(ks[16], (N_CONV_LAYERS, D_MODEL, D_MODEL), D_MODEL ** -0.5),
        "attn_w_qkv": nrm(ks[17], (N_ATTN_LAYERS, D_MODEL, 3 * D_MODEL), D_MODEL ** -0.5),
        "attn_w_out": nrm(ks[18], (N_ATTN_LAYERS, N_HEADS * V_DIM, D_MODEL), (N_HEADS * V_DIM) ** -0.5),
        "lambda_q1": nrm(ks[19], (N_ATTN_LAYERS, HEAD_DIM), 0.1),
        "lambda_k1": nrm(ks[20], (N_ATTN_LAYERS, HEAD_DIM), 0.1),
        "lambda_q2": nrm(ks[21], (N_ATTN_LAYERS, HEAD_DIM), 0.1),
        "lambda_k2": nrm(ks[22], (N_ATTN_LAYERS, HEAD_DIM), 0.1),
        "subln_w": 1.0 + nrm(ks[23], (N_ATTN_LAYERS, V_DIM), 0.02),
    }


def reference(x_prompt, x_sample, state_conv, cache_k, cache_v, page_table, c_prompt, c_sample,
              ada_w, ada_b, norm_pre, norm_post, ffn_w_in, ffn_w_out,
              conv_w_in, conv_w, conv_w_out, attn_w_qkv, attn_w_out,
              lambda_q1, lambda_k1, lambda_q2, lambda_k2, subln_w):
    xp, xs = x_prompt, x_sample
    conv_p, conv_s, k_p, v_p, k_s, v_s = [], [], [], [], [], []
    for l in range(DEPTH):
        mp = ada_modulation(c_prompt, ada_w[l], ada_b[l])
        ms = ada_modulation(c_sample, ada_w[l], ada_b[l])

        ffn1 = functools.partial(swiglu, w_in=ffn_w_in[l, 0], w_out=ffn_w_out[l, 0])
        xp, _ = sandwich(xp, mp[:, 0], norm_pre[l, 0], norm_post[l, 0], ffn1, 0.5)
        xs, _ = sandwich(xs, ms[:, 0], norm_pre[l, 0], norm_post[l, 0], ffn1, 0.5)

        i = l // N_MIXERS
        if l % N_MIXERS == 0:
            prev_p = jnp.zeros((xp.shape[0], CONV_W - 1, D_MODEL), xp.dtype)
            mix_p = functools.partial(short_conv_mixer, prev=prev_p, w_in=conv_w_in[i],
                                      conv_w=conv_w[i], w_out=conv_w_out[i])
            mix_s = functools.partial(short_conv_mixer, prev=state_conv[i], w_in=conv_w_in[i],
                                      conv_w=conv_w[i], w_out=conv_w_out[i])
            xp, st_p = sandwich(xp, mp[:, 1], norm_pre[l, 1], norm_post[l, 1], mix_p, 1.0)
            xs, st_s = sandwich(xs, ms[:, 1], norm_pre[l, 1], norm_post[l, 1], mix_s, 1.0)
            conv_p.append(st_p)
            conv_s.append(st_s)
        else:
            lam_init = 0.8 - 0.6 * math.exp(-0.3 * l)
            lam = diff_lambda(lambda_q1[i], lambda_k1[i], lambda_q2[i], lambda_k2[i], lam_init)

            def attn_prompt(h):
                q, k, v = split_qkv(h, attn_w_qkv[i])
                o = diff_attn_prompt(q, k, v, lam).astype(h.dtype)
                return diff_attn_out(o, subln_w[i], lam_init, attn_w_out[i]), (k, v)

            def attn_sample(h):
                q, k, v = split_qkv(h, attn_w_qkv[i])
                o = diff_attn_sample(q, k, v, cache_k, cache_v, i, page_table, lam).astype(h.dtype)
                return diff_attn_out(o, subln_w[i], lam_init, attn_w_out[i]), (k, v)

            xp, (kk, vv) = sandwich(xp, mp[:, 1], norm_pre[l, 1], norm_post[l, 1], attn_prompt, 1.0)
            k_p.append(kk)
            v_p.append(vv)
            xs, (kk, vv) = sandwich(xs, ms[:, 1], norm_pre[l, 1], norm_post[l, 1], attn_sample, 1.0)
            k_s.append(kk)
            v_s.append(vv)

        ffn2 = functools.partial(swiglu, w_in=ffn_w_in[l, 1], w_out=ffn_w_out[l, 1])
        xp, _ = sandwich(xp, mp[:, 2], norm_pre[l, 2], norm_post[l, 2], ffn2, 0.5)
        xs, _ = sandwich(xs, ms[:, 2], norm_pre[l, 2], norm_post[l, 2], ffn2, 0.5)

    new_conv_prompt = jnp.stack(conv_p)
    new_conv_sample = jnp.stack(conv_s)
    new_k_prompt = jnp.stack(k_p)
    new_v_prompt = jnp.stack(v_p)
    new_k_sample = jnp.stack(k_s)
    new_v_sample = jnp.stack(v_s)
    return (xp, xs, new_conv_prompt, new_conv_sample, new_k_prompt, new_v_prompt, new_k_sample, new_v_sample)
```

```python
import functools
import math

import jax
import jax.numpy as jnp
from jax import lax
from jax.experimental import pallas as pl
from jax.experimental.pallas import tpu as pltpu

F32 = jnp.float32
BF16 = jnp.bfloat16

D_MODEL = 4096
N_PROMPT_SEQ = 4
PROMPT_LEN = 2048
DEPTH = 2
N_SAMPLE_SEQ = 8
SAMPLE_LEN = 4
SAMPLE_PAD = 8
PAGE_SIZE = 128
N_PAGES = 16384 // PAGE_SIZE
D_FF = 11008
CONV_W = 3
N_HEADS = 16
HEAD_DIM = 128
V_DIM = 2 * HEAD_DIM
SCALE = HEAD_DIM ** -0.5
RMS_EPS = 1e-6
SUBLN_EPS = 1e-5
NEG_INF = -1e30
N_SUB = 3
N_MOD = 3
N_SEQ_ALL = 16

V7X_VMEM_BYTES = 64 * 2**20
VMEM_CAP = V7X_VMEM_BYTES - 6 * 2**20
FF_TILE = 256


def _cparams(n_grid, vmem_bytes):
    limit = int(min(VMEM_CAP, max(32 * 2**20, vmem_bytes + 8 * 2**20)))
    return pltpu.CompilerParams(dimension_semantics=("arbitrary",) * n_grid, vmem_limit_bytes=limit)


def _dot(a, b):
    return lax.dot_general(a, b, (((1,), (0,)), ((), ())), preferred_element_type=F32)


def _dot_nt(a, b):
    return lax.dot_general(a, b, (((1,), (1,)), ((), ())), preferred_element_type=F32)


def _sigmoid(x):
    return 1.0 / (1.0 + jnp.exp(-x))


def _rms(x, eps):
    return x * lax.rsqrt(jnp.mean(x * x, axis=-1, keepdims=True) + eps)


def _ada_kernel(c_ref, w_ref, b_ref, o_ref):
    c = c_ref[...]
    o_ref[...] = _dot(c * _sigmoid(c), w_ref[...]) + b_ref[...]


def _ada(c_all, ada_w, ada_b3, layer):
    n = N_SUB * N_MOD * D_MODEL
    tn = 512
    out = pl.pallas_call(
        _ada_kernel,
        out_shape=jax.ShapeDtypeStruct((N_SEQ_ALL, n), F32),
        grid=(n // tn,),
        in_specs=[
            pl.BlockSpec((N_SEQ_ALL, D_MODEL), lambda j: (0, 0)),
            pl.BlockSpec((None, D_MODEL, tn), lambda j: (layer, 0, j)),
            pl.BlockSpec((None, 1, tn), lambda j: (layer, 0, j)),
        ],
        out_specs=pl.BlockSpec((N_SEQ_ALL, tn), lambda j: (0, j)),
        compiler_params=_cparams(1, 2 * D_MODEL * tn * 4),
        name=f"ada_l{layer}",
    )(c_all, ada_w, ada_b3)
    return out.reshape(N_SEQ_ALL, N_SUB * N_MOD, D_MODEL)


class _Rows:
    def __init__(self, m, rows, rows_per_seq, seq0, act):
        self.m, self.rows, self.blocks_per_seq, self.seq0 = m, rows, rows_per_seq // rows, seq0
        self.act = act

    def seq(self, i):
        return i // self.blocks_per_seq + self.seq0


def _modulate(x, mod_ref, sub, g_ref, g_row):
    shift = mod_ref[N_MOD * sub:N_MOD * sub + 1, :]
    scale = mod_ref[N_MOD * sub + 1:N_MOD * sub + 2, :]
    return _rms(x, RMS_EPS) * g_ref[g_row:g_row + 1, :] * (1.0 + scale) + shift


def _prenorm_kernel(x_ref, mod_ref, g_ref, h_ref, *, sub, g_row):
    h_ref[...] = _modulate(x_ref[...], mod_ref, sub, g_ref, g_row).astype(h_ref.dtype)


def _prenorm(geo, x, mod, g_pre, sub, g_row):
    r = geo.rows
    return pl.pallas_call(
        functools.partial(_prenorm_kernel, sub=sub, g_row=g_row),
        out_shape=jax.ShapeDtypeStruct((geo.m, D_MODEL), geo.act),
        grid=(geo.m // r,),
        in_specs=[
            pl.BlockSpec((r, D_MODEL), lambda i: (i, 0)),
            pl.BlockSpec((None, N_SUB * N_MOD, D_MODEL), lambda i: (geo.seq(i), 0, 0)),
            pl.BlockSpec(g_pre.shape, lambda i: (0, 0)),
        ],
        out_specs=pl.BlockSpec((r, D_MODEL), lambda i: (i, 0)),
        compiler_params=_cparams(1, 2 * r * D_MODEL * 6),
        name="prenorm",
    )(x, mod, g_pre)


def _post_kernel(x_ref, y_ref, mod_ref, gpost_ref, *rest, sub, g_row, coef, nxt):
    gate = mod_ref[N_MOD * sub + 2:N_MOD * sub + 3, :]
    y = _rms(y_ref[...], RMS_EPS) * gpost_ref[g_row:g_row + 1, :]
    x_new = x_ref[...] + (coef * gate) * y
    if nxt is None:
        (xo_ref,) = rest
        xo_ref[...] = x_new
    else:
        modn_ref, gpre_ref, xo_ref, h_ref = rest
        xo_ref[...] = x_new
        h_ref[...] = _modulate(x_new, modn_ref, nxt[0], gpre_ref, nxt[1]).astype(h_ref.dtype)


def _post(geo, x, y, mod, g_post, sub, g_row, coef, nxt=None, mod_next=None, g_pre=None):
    r = geo.rows
    row_spec = pl.BlockSpec((r, D_MODEL), lambda i: (i, 0))
    mod_spec = pl.BlockSpec((None, N_SUB * N_MOD, D_MODEL), lambda i: (geo.seq(i), 0, 0))
    in_specs = [row_spec, row_spec, mod_spec, pl.BlockSpec(g_post.shape, lambda i: (0, 0))]
    args = [x, y, mod, g_post]
    out_shape = [jax.ShapeDtypeStruct((geo.m, D_MODEL), F32)]
    out_specs = [row_spec]
    if nxt is not None:
        in_specs += [mod_spec, pl.BlockSpec(g_pre.shape, lambda i: (0, 0))]
        args += [mod_next, g_pre]
        out_shape.append(jax.ShapeDtypeStruct((geo.m, D_MODEL), geo.act))
        out_specs.append(row_spec)
    res = pl.pallas_call(
        functools.partial(_post_kernel, sub=sub, g_row=g_row, coef=coef, nxt=nxt),
        out_shape=out_shape,
        grid=(geo.m // r,),
        in_specs=in_specs,
        out_specs=out_specs,
        compiler_params=_cparams(1, 2 * r * D_MODEL * 14),
        name="post",
    )(*args)
    return (res[0], res[1]) if nxt is not None else (res[0], None)


def _w_spec(w, lead, k, tn, col_block0):
    none = (None,) * len(lead)
    return pl.BlockSpec(none + (k, tn), lambda i, j: lead + (0, j + col_block0))


def _mm_kernel(x_ref, w_ref, o_ref):
    o_ref[...] = _dot(x_ref[...], w_ref[...]).astype(o_ref.dtype)


def _mm(x, w, lead, col0, n, tm, tn, out_dtype, name, single_buffer_x=False):
    m, k = x.shape
    x_kw = dict(pipeline_mode=pl.Buffered(1)) if single_buffer_x else {}
    x_bytes = tm * k * x.dtype.itemsize * (1 if single_buffer_x else 2)
    vmem = x_bytes + 2 * k * tn * 4 + 2 * tm * tn * 4
    return pl.pallas_call(
        _mm_kernel,
        out_shape=jax.ShapeDtypeStruct((m, n), out_dtype),
        grid=(m // tm, n // tn),
        in_specs=[pl.BlockSpec((tm, k), lambda i, j: (i, 0), **x_kw), _w_spec(w, lead, k, tn, col0 // tn)],
        out_specs=pl.BlockSpec((tm, tn), lambda i, j: (i, j)),
        compiler_params=_cparams(2, vmem),
        name=name,
    )(x, w)


def _swiglu_kernel(h_ref, wg_ref, wu_ref, a_ref):
    h = h_ref[...]
    g = _dot(h, wg_ref[...])
    u = _dot(h, wu_ref[...])
    a_ref[...] = (g * _sigmoid(g) * u).astype(a_ref.dtype)


def _mm_swiglu(h, w_in, lead, tm):
    m, k = h.shape
    tn = FF_TILE
    vmem = 2 * tm * k * 2 + 4 * k * tn * 4 + 2 * tm * tn * 2 + 3 * tm * tn * 4
    return pl.pallas_call(
        _swiglu_kernel,
        out_shape=jax.ShapeDtypeStruct((m, D_FF), BF16),
        grid=(m // tm, D_FF // tn),
        in_specs=[pl.BlockSpec((tm, k), lambda i, j: (i, 0)), _w_spec(w_in, lead, k, tn, 0), _w_spec(w_in, lead, k, tn, D_FF // tn)],
        out_specs=pl.BlockSpec((tm, tn), lambda i, j: (i, j)),
        compiler_params=_cparams(2, vmem),
        name="ffn_in",
    )(h, w_in, w_in)


def _conv_in_kernel(h_ref, wb_ref, wc_ref, wx_ref, b_ref, u_ref):
    h = h_ref[...]
    b_ref[...] = _dot(h, wb_ref[...])
    u_ref[...] = _dot(h, wc_ref[...]) * _dot(h, wx_ref[...])


def _mm_conv_in(h, w_in, lead, tm):
    m, k = h.shape
    tn = 256
    nb = D_MODEL // tn
    vmem = 2 * tm * k * 2 + 6 * k * tn * 4 + 4 * tm * tn * 4 + 3 * tm * tn * 4
    out = jax.ShapeDtypeStruct((m, D_MODEL), F32)
    return pl.pallas_call(
        _conv_in_kernel,
        out_shape=[out, out],
        grid=(m // tm, nb),
        in_specs=[pl.BlockSpec((tm, k), lambda i, j: (i, 0)),
                  _w_spec(w_in, lead, k, tn, 0), _w_spec(w_in, lead, k, tn, nb), _w_spec(w_in, lead, k, tn, 2 * nb)],
        out_specs=[pl.BlockSpec((tm, tn), lambda i, j: (i, j))] * 2,
        compiler_params=_cparams(2, vmem),
        name="conv_in",
    )(h, w_in, w_in, w_in)


def _conv_kernel(u_ref, halo_ref, b_ref, cw_ref, z_ref, *, blocks_per_seq, zero_first):
    u = u_ref[...]
    halo = halo_ref[...]
    if zero_first:
        first = (pl.program_id(0) % blocks_per_seq) == 0
        halo = jnp.where(first, 0.0, halo)
    row = lax.broadcasted_iota(jnp.int32, u.shape, 0)
    prev1 = halo[7:8, :]
    prev2 = halo[6:7, :]
    u1 = jnp.where(row == 0, prev1, pltpu.roll(u, 1, 0))
    u2 = jnp.where(row == 0, prev2, jnp.where(row == 1, prev1, pltpu.roll(u, 2, 0)))
    conv = cw_ref[0:1, :] * u2 + cw_ref[1:2, :] * u1 + cw_ref[2:3, :] * u
    z_ref[...] = (b_ref[...] * conv).astype(z_ref.dtype)


def _conv(geo, u, halo_src, b, conv_w, zero_first):
    r = geo.rows
    tn = 1024
    if zero_first:
        halo_map = lambda i, j: (jnp.maximum(i * (r // 8) - 1, 0), j)
    else:
        halo_map = lambda i, j: (i, j)
    return pl.pallas_call(
        functools.partial(_conv_kernel, blocks_per_seq=geo.blocks_per_seq, zero_first=zero_first),
        out_shape=jax.ShapeDtypeStruct((geo.m, D_MODEL), geo.act),
        grid=(geo.m // r, D_MODEL // tn),
        in_specs=[pl.BlockSpec((r, tn), lambda i, j: (i, j)), pl.BlockSpec((8, tn), halo_map),
                  pl.BlockSpec((r, tn), lambda i, j: (i, j)), pl.BlockSpec((CONV_W, tn), lambda i, j: (0, j))],
        out_specs=pl.BlockSpec((r, tn), lambda i, j: (i, j)),
        compiler_params=_cparams(2, 2 * r * tn * 24),
        name="conv",
    )(u, halo_src, b, conv_w)


def _lambda(lq1_ref, lk1_ref, lq2_ref, lk2_ref, lam_init):
    a = jnp.sum(lq1_ref[...] * lk1_ref[...], axis=-1, keepdims=True)
    b = jnp.sum(lq2_ref[...] * lk2_ref[...], axis=-1, keepdims=True)
    return jnp.exp(a) - jnp.exp(b) + lam_init


def _attn_prompt_kernel(lq1_ref, lk1_ref, lq2_ref, lk2_ref, sub_ref, q_ref, k_ref, v_ref, o_ref,
                        m_ref, l_ref, acc_ref, *, tq, lam_init):
    qi = pl.program_id(2)
    lam = _lambda(lq1_ref, lk1_ref, lq2_ref, lk2_ref, lam_init)
    q = q_ref[...]
    qs = (q[:, :HEAD_DIM], q[:, HEAD_DIM:])
    m_ref[...] = jnp.full(m_ref.shape, NEG_INF, F32)
    l_ref[...] = jnp.zeros(l_ref.shape, F32)
    acc_ref[...] = jnp.zeros(acc_ref.shape, F32)

    def block(kb, masked):
        start = pl.multiple_of(kb * tq, tq)
        k = k_ref[pl.ds(start, tq), :]
        v = v_ref[pl.ds(start, tq), :]
        for j in range(2):
            s = _dot_nt(qs[j], k[:, j * HEAD_DIM:(j + 1) * HEAD_DIM]) * SCALE
            if masked:
                r = lax.broadcasted_iota(jnp.int32, s.shape, 0)
                c = lax.broadcasted_iota(jnp.int32, s.shape, 1)
                s = jnp.where(c <= r, s, NEG_INF)
            m_old = m_ref[j]
            m_new = jnp.maximum(m_old, jnp.max(s, axis=-1, keepdims=True))
            corr = jnp.exp(m_old - m_new)
            p = jnp.exp(s - m_new)
            l_ref[j] = l_ref[j] * corr + jnp.sum(p, axis=-1, keepdims=True)
            acc_ref[j] = acc_ref[j] * corr + _dot(p, v)
            m_ref[j] = m_new

    def body(kb, carry):
        block(kb, False)
        return carry

    lax.fori_loop(0, qi, body, 0)
    block(qi, True)
    o = acc_ref[0] / l_ref[0] - lam * (acc_ref[1] / l_ref[1])
    o = _rms(o, SUBLN_EPS) * sub_ref[...] * (1.0 - lam_init)
    o_ref[...] = o.astype(o_ref.dtype)


def _attn_prompt(q, k, v, lam_vecs, subln, lam_init):
    tq = 256
    nq = PROMPT_LEN // tq
    m = q.shape[0]
    vec = pl.BlockSpec((1, HEAD_DIM), lambda b, h, i: (0, 0))
    kv_spec = pl.BlockSpec((PROMPT_LEN, V_DIM), lambda b, h, i: (b, h))
    return pl.pallas_call(
        functools.partial(_attn_prompt_kernel, tq=tq, lam_init=lam_init),
        out_shape=jax.ShapeDtypeStruct((m, D_MODEL), BF16),
        grid=(N_PROMPT_SEQ, N_HEADS, nq),
        in_specs=[vec, vec, vec, vec, pl.BlockSpec((1, V_DIM), lambda b, h, i: (0, 0)),
                  pl.BlockSpec((tq, V_DIM), lambda b, h, i: (b * nq + i, h)), kv_spec, kv_spec],
        out_specs=pl.BlockSpec((tq, V_DIM), lambda b, h, i: (b * nq + i, h)),
        scratch_shapes=[pltpu.VMEM((2, tq, 1), F32), pltpu.VMEM((2, tq, 1), F32), pltpu.VMEM((2, tq, V_DIM), F32)],
        compiler_params=_cparams(3, 4 * PROMPT_LEN * V_DIM * 4 + 8 * tq * V_DIM * 4),
        name="attn_prompt",
    )(*lam_vecs, subln, q, k, v)


N_QROWS = 2 * SAMPLE_LEN * N_HEADS


def _attn_sample_kernel(pt_ref, lq1_ref, lk1_ref, lq2_ref, lk2_ref, sub_ref, q_ref, kn_ref, vn_ref, ck_ref, cv_ref,
                        o_ref, qm_ref, m_ref, l_ref, acc_ref, *, lam_init):
    p = pl.program_id(1)

    @pl.when(p == 0)
    def _():
        r = lax.broadcasted_iota(jnp.int32, qm_ref.shape, 0)
        c = lax.broadcasted_iota(jnp.int32, qm_ref.shape, 1)
        own = (c // HEAD_DIM) == (r % N_HEADS) * 2 + r // (SAMPLE_LEN * N_HEADS)
        qm_ref[...] = jnp.where(own, q_ref[...], 0.0)
        m_ref[...] = jnp.full(m_ref.shape, NEG_INF, F32)
        l_ref[...] = jnp.zeros(l_ref.shape, F32)
        acc_ref[...] = jnp.zeros(acc_ref.shape, F32)

    def step(k, v, masked):
        s = _dot_nt(qm_ref[...], k) * SCALE
        if masked:
            r = lax.broadcasted_iota(jnp.int32, s.shape, 0)
            c = lax.broadcasted_iota(jnp.int32, s.shape, 1)
            s = jnp.where(c <= (r // N_HEADS) % SAMPLE_LEN, s, NEG_INF)
        m_old = m_ref[...]
        m_new = jnp.maximum(m_old, jnp.max(s, axis=-1, keepdims=True))
        corr = jnp.exp(m_old - m_new)
        pr = jnp.exp(s - m_new)
        l_ref[...] = l_ref[...] * corr + jnp.sum(pr, axis=-1, keepdims=True)
        acc_ref[...] = acc_ref[...] * corr + _dot(pr, v)
        m_ref[...] = m_new

    @pl.when(p < N_PAGES)
    def _():
        step(ck_ref[...], cv_ref[...], False)

    @pl.when(p == N_PAGES)
    def _():
        step(kn_ref[...], vn_ref[...], True)
        lam = _lambda(lq1_ref, lk1_ref, lq2_ref, lk2_ref, lam_init)
        o = acc_ref[...] / l_ref[...]
        half = N_QROWS // 2
        d = o[:half] - lam * o[half:]
        r = lax.broadcasted_iota(jnp.int32, d.shape, 0)
        c = lax.broadcasted_iota(jnp.int32, d.shape, 1)
        d = jnp.where((c // V_DIM) == (r % N_HEADS), d, 0.0)
        ms = jnp.sum(d * d, axis=-1, keepdims=True) * (1.0 / V_DIM)
        d = d * lax.rsqrt(ms + SUBLN_EPS) * sub_ref[...] * (1.0 - lam_init)
        o_ref[...] = jnp.zeros(o_ref.shape, o_ref.dtype)
        for t in range(SAMPLE_LEN):
            row = jnp.sum(d[t * N_HEADS:(t + 1) * N_HEADS], axis=0, keepdims=True)
            o_ref[t:t + 1, :] = row.astype(o_ref.dtype)


def _attn_sample(q_rep, k_new, v_new, cache_k, cache_v, page_table, lam_vecs, subln_tiled, lam_init):
    vec = pl.BlockSpec((1, HEAD_DIM), lambda b, p, pt: (0, 0))
    new_spec = pl.BlockSpec((None, PAGE_SIZE, D_MODEL), lambda b, p, pt: (b, 0, 0))
    page_spec = pl.BlockSpec((None, None, PAGE_SIZE, D_MODEL),
                             lambda b, p, pt: (0, pt[b, jnp.minimum(p, N_PAGES - 1)], 0, 0))
    grid_spec = pltpu.PrefetchScalarGridSpec(
        num_scalar_prefetch=1,
        grid=(N_SAMPLE_SEQ, N_PAGES + 1),
        in_specs=[vec, vec, vec, vec, pl.BlockSpec((1, D_MODEL), lambda b, p, pt: (0, 0)),
                  pl.BlockSpec((None, N_QROWS, D_MODEL), lambda b, p, pt: (b, 0, 0)),
                  new_spec, new_spec, page_spec, page_spec],
        out_specs=pl.BlockSpec((SAMPLE_PAD, D_MODEL), lambda b, p, pt: (b, 0)),
        scratch_shapes=[pltpu.VMEM((N_QROWS, D_MODEL), F32), pltpu.VMEM((N_QROWS, 1), F32),
                        pltpu.VMEM((N_QROWS, 1), F32), pltpu.VMEM((N_QROWS, D_MODEL), F32)],
    )
    return pl.pallas_call(
        functools.partial(_attn_sample_kernel, lam_init=lam_init),
        out_shape=jax.ShapeDtypeStruct((N_SAMPLE_SEQ * SAMPLE_PAD, D_MODEL), F32),
        grid_spec=grid_spec,
        compiler_params=_cparams(2, 14 * PAGE_SIZE * D_MODEL * 4),
        name="attn_sample",
    )(page_table, *lam_vecs, subln_tiled, q_rep, k_new, v_new, cache_k, cache_v)


def _run_path(geo, tm, x, mods, p, mixer_conv, mixer_attn):
    norm_pre, norm_post = p["norm_pre"], p["norm_post"]
    h = _prenorm(geo, x, mods[0], norm_pre, 0, 0)
    aux = {}
    for l in range(DEPTH):
        for sub in range(N_SUB):
            g_row = l * N_SUB + sub
            if sub == 1:
                if l % 2 == 0:
                    y, aux[l] = mixer_conv(l // 2, h)
                else:
                    y, aux[l] = mixer_attn(l // 2, l, h)
                coef = 1.0
            else:
                a = _mm_swiglu(h, p["ffn_w_in"], (l, sub // 2), tm)
                y = _mm(a, p["ffn_w_out"], (l, sub // 2), 0, D_MODEL, tm, 256, F32, "ffn_out", single_buffer_x=tm >= 1024)
                coef = 0.5
            if sub + 1 < N_SUB:
                nxt, mod_next = (sub + 1, g_row + 1), mods[l]
            elif l + 1 < DEPTH:
                nxt, mod_next = (0, g_row + 1), mods[l + 1]
            else:
                nxt, mod_next = None, None
            x, h = _post(geo, x, y, mods[l], norm_post, sub, g_row, coef, nxt, mod_next, norm_pre)
    return x, aux


def kernel(x_prompt, x_sample, state_conv, cache_k, cache_v, page_table, c_prompt, c_sample,
           ada_w, ada_b, norm_pre, norm_post, ffn_w_in, ffn_w_out,
           conv_w_in, conv_w, conv_w_out, attn_w_qkv, attn_w_out,
           lambda_q1, lambda_k1, lambda_q2, lambda_k2, subln_w):
    p = dict(norm_pre=norm_pre.reshape(DEPTH * N_SUB, D_MODEL), norm_post=norm_post.reshape(DEPTH * N_SUB, D_MODEL),
             ffn_w_in=ffn_w_in, ffn_w_out=ffn_w_out)
    c_all = jnp.zeros((N_SEQ_ALL, D_MODEL), F32).at[:N_PROMPT_SEQ].set(c_prompt)
    c_all = c_all.at[N_PROMPT_SEQ:N_PROMPT_SEQ + N_SAMPLE_SEQ].set(c_sample)
    ada_b3 = ada_b.reshape(DEPTH, 1, N_SUB * N_MOD * D_MODEL)
    mods = [_ada(c_all, ada_w, ada_b3, l) for l in range(DEPTH)]

    m_p = N_PROMPT_SEQ * PROMPT_LEN
    m_s = N_SAMPLE_SEQ * SAMPLE_PAD
    geo_p = _Rows(m_p, 256, PROMPT_LEN, 0, BF16)
    geo_s = _Rows(m_s, SAMPLE_PAD, SAMPLE_PAD, N_PROMPT_SEQ, F32)
    tm_p, tm_s = 1024, m_s
    n_kv = cache_k.shape[1]
    cache_k4 = cache_k.reshape(cache_k.shape[0], n_kv, PAGE_SIZE, D_MODEL)
    cache_v4 = cache_v.reshape(cache_v.shape[0], n_kv, PAGE_SIZE, D_MODEL)

    def lam_vecs(i):
        return [a[i].reshape(1, HEAD_DIM) for a in (lambda_q1, lambda_k1, lambda_q2, lambda_k2)]

    def conv_mixer(geo, tm, prompt):
        def run(i, h):
            b, u = _mm_conv_in(h, conv_w_in, (i,), tm)
            if prompt:
                halo_src = u
                new_state = u.reshape(N_PROMPT_SEQ, PROMPT_LEN, D_MODEL)[:, PROMPT_LEN - (CONV_W - 1):]
            else:
                halo_src = jnp.pad(state_conv[i], ((0, 0), (SAMPLE_PAD - (CONV_W - 1), 0), (0, 0))).reshape(m_s, D_MODEL)
                new_state = u.reshape(N_SAMPLE_SEQ, SAMPLE_PAD, D_MODEL)[:, SAMPLE_LEN - (CONV_W - 1):SAMPLE_LEN]
            z = _conv(geo, u, halo_src, b, conv_w[i], zero_first=prompt)
            y = _mm(z, conv_w_out, (i,), 0, D_MODEL, tm, 512, F32, "conv_out")
            return y, new_state
        return run

    def attn_mixer(geo, tm, prompt):
        def run(i, l, h):
            lam_init = 0.8 - 0.6 * math.exp(-0.3 * l)
            q = _mm(h, attn_w_qkv, (i,), 0, D_MODEL, tm, 512, F32, "attn_q")
            k = _mm(h, attn_w_qkv, (i,), D_MODEL, D_MODEL, tm, 512, F32, "attn_k")
            v = _mm(h, attn_w_qkv, (i,), 2 * D_MODEL, D_MODEL, tm, 512, F32, "attn_v")
            if prompt:
                o = _attn_prompt(q, k, v, lam_vecs(i), subln_w[i].reshape(1, V_DIM), lam_init)
                kv = (k.reshape(N_PROMPT_SEQ, PROMPT_LEN, N_HEADS, 2, HEAD_DIM),
                      v.reshape(N_PROMPT_SEQ, PROMPT_LEN, N_HEADS, V_DIM))
            else:
                q3 = q.reshape(N_SAMPLE_SEQ, SAMPLE_PAD, D_MODEL)[:, :SAMPLE_LEN]
                q_rep = jnp.broadcast_to(q3[:, None, :, None, :], (N_SAMPLE_SEQ, 2, SAMPLE_LEN, N_HEADS, D_MODEL))
                q_rep = q_rep.reshape(N_SAMPLE_SEQ, N_QROWS, D_MODEL)
                pad = ((0, 0), (0, PAGE_SIZE - SAMPLE_PAD), (0, 0))
                k_new = jnp.pad(k.reshape(N_SAMPLE_SEQ, SAMPLE_PAD, D_MODEL), pad)
                v_new = jnp.pad(v.reshape(N_SAMPLE_SEQ, SAMPLE_PAD, D_MODEL), pad)
                sub_t = jnp.tile(subln_w[i], N_HEADS).reshape(1, D_MODEL)
                o = _attn_sample(q_rep, k_new, v_new, cache_k4, cache_v4, page_table, lam_vecs(i), sub_t, lam_init)
                kv = (k.reshape(N_SAMPLE_SEQ, SAMPLE_PAD, N_HEADS, 2, HEAD_DIM)[:, :SAMPLE_LEN],
                      v.reshape(N_SAMPLE_SEQ, SAMPLE_PAD, N_HEADS, V_DIM)[:, :SAMPLE_LEN])
            y = _mm(o, attn_w_out, (i,), 0, D_MODEL, tm, 512, F32, "attn_out")
            return y, kv
        return run

    xp = x_prompt.reshape(m_p, D_MODEL)
    xs = jnp.pad(x_sample, ((0, 0), (0, SAMPLE_PAD - SAMPLE_LEN), (0, 0))).reshape(m_s, D_MODEL)
    yp, aux_p = _run_path(geo_p, tm_p, xp, mods, p, conv_mixer(geo_p, tm_p, True), attn_mixer(geo_p, tm_p, True))
    ys, aux_s = _run_path(geo_s, tm_s, xs, mods, p, conv_mixer(geo_s, tm_s, False), attn_mixer(geo_s, tm_s, False))

    conv_layers = [l for l in range(DEPTH) if l % 2 == 0]
    attn_layers = [l for l in range(DEPTH) if l % 2 == 1]
    y_prompt = yp.reshape(N_PROMPT_SEQ, PROMPT_LEN, D_MODEL)
    y_sample = ys.reshape(N_SAMPLE_SEQ, SAMPLE_PAD, D_MODEL)[:, :SAMPLE_LEN]
    return (y_prompt, y_sample,
            jnp.stack([aux_p[l] for l in conv_layers]), jnp.stack([aux_s[l] for l in conv_layers]),
            jnp.stack([aux_p[l][0] for l in attn_layers]), jnp.stack([aux_p[l][1] for l in attn_layers]),
            jnp.stack([aux_s[l][0] for l in attn_layers]), jnp.stack([aux_s[l][1] for l in attn_layers]))
```

```python
import functools
import math

import jax
import jax.numpy as jnp
from jax import lax
from jax.experimental import pallas as pl
from jax.experimental.pallas import tpu as pltpu

F32 = jnp.float32
BF16 = jnp.bfloat16

D_MODEL = 4096
N_PROMPT_SEQ = 4
PROMPT_LEN = 2048
DEPTH = 2
N_SAMPLE_SEQ = 8
SAMPLE_LEN = 4
SAMPLE_PAD = 8
PAGE_SIZE = 128
N_PAGES = 16384 // PAGE_SIZE
D_FF = 11008
CONV_W = 3
N_HEADS = 16
HEAD_DIM = 128
V_DIM = 2 * HEAD_DIM
SCALE = HEAD_DIM ** -0.5
RMS_EPS = 1e-6
SUBLN_EPS = 1e-5
NEG_INF = -1e30
N_SUB = 3
N_MOD = 3
N_SEQ_ALL = 16

V7X_VMEM_BYTES = 64 * 2**20
VMEM_CAP = V7X_VMEM_BYTES - 6 * 2**20
FF_TILE = 256


def _cparams(n_grid, vmem_bytes):
    limit = int(min(VMEM_CAP, max(32 * 2**20, vmem_bytes + 8 * 2**20)))
    return pltpu.CompilerParams(dimension_semantics=("arbitrary",) * n_grid, vmem_limit_bytes=limit)


def _dot(a, b):
    return lax.dot_general(a, b, (((1,), (0,)), ((), ())), preferred_element_type=F32)


def _dot_nt(a, b):
    return lax.dot_general(a, b, (((1,), (1,)), ((), ())), preferred_element_type=F32)


def _sigmoid(x):
    return 1.0 / (1.0 + jnp.exp(-x))


def _rms(x, eps):
    return x * lax.rsqrt(jnp.mean(x * x, axis=-1, keepdims=True) + eps)


def _ada_kernel(c_ref, w_ref, b_ref, o_ref):
    c = c_ref[...]
    o_ref[...] = _dot(c * _sigmoid(c), w_ref[...]) + b_ref[...]


def _ada(c_all, ada_w, ada_b3, layer):
    n = N_SUB * N_MOD * D_MODEL
    tn = 512
    out = pl.pallas_call(
        _ada_kernel,
        out_shape=jax.ShapeDtypeStruct((N_SEQ_ALL, n), F32),
        grid=(n // tn,),
        in_specs=[
            pl.BlockSpec((N_SEQ_ALL, D_MODEL), lambda j: (0, 0)),
            pl.BlockSpec((None, D_MODEL, tn), lambda j: (layer, 0, j)),
            pl.BlockSpec((None, 1, tn), lambda j: (layer, 0, j)),
        ],
        out_specs=pl.BlockSpec((N_SEQ_ALL, tn), lambda j: (0, j)),
        compiler_params=_cparams(1, 2 * D_MODEL * tn * 4),
        name=f"ada_l{layer}",
    )(c_all, ada_w, ada_b3)
    return out.reshape(N_SEQ_ALL, N_SUB * N_MOD, D_MODEL)


class _Rows:
    def __init__(self, m, rows, rows_per_seq, seq0, act):
        self.m, self.rows, self.blocks_per_seq, self.seq0 = m, rows, rows_per_seq // rows, seq0
        self.act = act

    def seq(self, i):
        return i // self.blocks_per_seq + self.seq0


def _modulate(x, mod_ref, sub, g_ref, g_row):
    shift = mod_ref[N_MOD * sub:N_MOD * sub + 1, :]
    scale = mod_ref[N_MOD * sub + 1:N_MOD * sub + 2, :]
    return _rms(x, RMS_EPS) * g_ref[g_row:g_row + 1, :] * (1.0 + scale) + shift


def _prenorm_kernel(x_ref, mod_ref, g_ref, h_ref, *, sub, g_row):
    h_ref[...] = _modulate(x_ref[...], mod_ref, sub, g_ref, g_row).astype(h_ref.dtype)


def _prenorm(geo, x, mod, g_pre, sub, g_row):
    r = geo.rows
    return pl.pallas_call(
        functools.partial(_prenorm_kernel, sub=sub, g_row=g_row),
        out_shape=jax.ShapeDtypeStruct((geo.m, D_MODEL), geo.act),
        grid=(geo.m // r,),
        in_specs=[
            pl.BlockSpec((r, D_MODEL), lambda i: (i, 0)),
            pl.BlockSpec((None, N_SUB * N_MOD, D_MODEL), lambda i: (geo.seq(i), 0, 0)),
            pl.BlockSpec(g_pre.shape, lambda i: (0, 0)),
        ],
        out_specs=pl.BlockSpec((r, D_MODEL), lambda i: (i, 0)),
        compiler_params=_cparams(1, 2 * r * D_MODEL * 6),
        name="prenorm",
    )(x, mod, g_pre)


def _post_kernel(x_ref, y_ref, mod_ref, gpost_ref, *rest, sub, g_row, coef, nxt):
    gate = mod_ref[N_MOD * sub + 2:N_MOD * sub + 3, :]
    y = _rms(y_ref[...], RMS_EPS) * gpost_ref[g_row:g_row + 1, :]
    x_new = x_ref[...] + (coef * gate) * y
    if nxt is None:
        (xo_ref,) = rest
        xo_ref[...] = x_new
    else:
        modn_ref, gpre_ref, xo_ref, h_ref = rest
        xo_ref[...] = x_new
        h_ref[...] = _modulate(x_new, modn_ref, nxt[0], gpre_ref, nxt[1]).astype(h_ref.dtype)


def _post(geo, x, y, mod, g_post, sub, g_row, coef, nxt=None, mod_next=None, g_pre=None):
    r = geo.rows
    row_spec = pl.BlockSpec((r, D_MODEL), lambda i: (i, 0))
    mod_spec = pl.BlockSpec((None, N_SUB * N_MOD, D_MODEL), lambda i: (geo.seq(i), 0, 0))
    in_specs = [row_spec, row_spec, mod_spec, pl.BlockSpec(g_post.shape, lambda i: (0, 0))]
    args = [x, y, mod, g_post]
    out_shape = [jax.ShapeDtypeStruct((geo.m, D_MODEL), F32)]
    out_specs = [row_spec]
    if nxt is not None:
        in_specs += [mod_spec, pl.BlockSpec(g_pre.shape, lambda i: (0, 0))]
        args += [mod_next, g_pre]
        out_shape.append(jax.ShapeDtypeStruct((geo.m, D_MODEL), geo.act))
        out_specs.append(row_spec)
    res = pl.pallas_call(
        functools.partial(_post_kernel, sub=sub, g_row=g_row, coef=coef, nxt=nxt),
        out_shape=out_shape,
        grid=(geo.m // r,),
        in_specs=in_specs,
        out_specs=out_specs,
        compiler_params=_cparams(1, 2 * r * D_MODEL * 14),
        name="post",
    )(*args)
    return (res[0], res[1]) if nxt is not None else (res[0], None)


def _w_spec(w, lead, k, tn, col_block0):
    none = (None,) * len(lead)
    return pl.BlockSpec(none + (k, tn), lambda i, j: lead + (0, j + col_block0))


def _mm_kernel(x_ref, w_ref, o_ref):
    o_ref[...] = _dot(x_ref[...], w_ref[...]).astype(o_ref.dtype)


def _mm(x, w, lead, col0, n, tm, tn, out_dtype, name, single_buffer_x=False):
    m, k = x.shape
    x_kw = dict(pipeline_mode=pl.Buffered(1)) if single_buffer_x else {}
    x_bytes = tm * k * x.dtype.itemsize * (1 if single_buffer_x else 2)
    vmem = x_bytes + 2 * k * tn * 4 + 2 * tm * tn * 4
    return pl.pallas_call(
        _mm_kernel,
        out_shape=jax.ShapeDtypeStruct((m, n), out_dtype),
        grid=(m // tm, n // tn),
        in_specs=[pl.BlockSpec((tm, k), lambda i, j: (i, 0), **x_kw), _w_spec(w, lead, k, tn, col0 // tn)],
        out_specs=pl.BlockSpec((tm, tn), lambda i, j: (i, j)),
        compiler_params=_cparams(2, vmem),
        name=name,
    )(x, w)


def _swiglu_kernel(h_ref, wg_ref, wu_ref, a_ref):
    h = h_ref[...]
    g = _dot(h, wg_ref[...])
    u = _dot(h, wu_ref[...])
    a_ref[...] = (g * _sigmoid(g) * u).astype(a_ref.dtype)


def _mm_swiglu(h, w_in, lead, tm):
    m, k = h.shape
    tn = FF_TILE
    vmem = 2 * tm * k * 2 + 4 * k * tn * 4 + 2 * tm * tn * 2 + 3 * tm * tn * 4
    return pl.pallas_call(
        _swiglu_kernel,
        out_shape=jax.ShapeDtypeStruct((m, D_FF), BF16),
        grid=(m // tm, D_FF // tn),
        in_specs=[pl.BlockSpec((tm, k), lambda i, j: (i, 0)), _w_spec(w_in, lead, k, tn, 0), _w_spec(w_in, lead, k, tn, D_FF // tn)],
        out_specs=pl.BlockSpec((tm, tn), lambda i, j: (i, j)),
        compiler_params=_cparams(2, vmem),
        name="ffn_in",
    )(h, w_in, w_in)


def _conv_in_kernel(h_ref, wb_ref, wc_ref, wx_ref, b_ref, u_ref):
    h = h_ref[...]
    b_ref[...] = _dot(h, wb_ref[...])
    u_ref[...] = _dot(h, wc_ref[...]) * _dot(h, wx_ref[...])


def _mm_conv_in(h, w_in, lead, tm):
    m, k = h.shape
    tn = 256
    nb = D_MODEL // tn
    vmem = 2 * tm * k * 2 + 6 * k * tn * 4 + 4 * tm * tn * 4 + 3 * tm * tn * 4
    out = jax.ShapeDtypeStruct((m, D_MODEL), F32)
    return pl.pallas_call(
        _conv_in_kernel,
        out_shape=[out, out],
        grid=(m // tm, nb),
        in_specs=[pl.BlockSpec((tm, k), lambda i, j: (i, 0)),
                  _w_spec(w_in, lead, k, tn, 0), _w_spec(w_in, lead, k, tn, nb), _w_spec(w_in, lead, k, tn, 2 * nb)],
        out_specs=[pl.BlockSpec((tm, tn), lambda i, j: (i, j))] * 2,
        compiler_params=_cparams(2, vmem),
        name="conv_in",
    )(h, w_in, w_in, w_in)


def _conv_kernel(u_ref, halo_ref, b_ref, cw_ref, z_ref, *, blocks_per_seq, zero_first):
    u = u_ref[...]
    halo = halo_ref[...]
    if zero_first:
        first = (pl.program_id(0) % blocks_per_seq) == 0
        halo = jnp.where(first, 0.0, halo)
    row = lax.broadcasted_iota(jnp.int32, u.shape, 0)
    prev1 = halo[7:8, :]
    prev2 = halo[6:7, :]
    u1 = jnp.where(row == 0, prev1, pltpu.roll(u, 1, 0))
    u2 = jnp.where(row == 0, prev2, jnp.where(row == 1, prev1, pltpu.roll(u, 2, 0)))
    conv = cw_ref[0:1, :] * u2 + cw_ref[1:2, :] * u1 + cw_ref[2:3, :] * u
    z_ref[...] = (b_ref[...] * conv).astype(z_ref.dtype)


def _conv(geo, u, halo_src, b, conv_w, zero_first):
    r = geo.rows
    tn = 1024
    if zero_first:
        halo_map = lambda i, j: (jnp.maximum(i * (r // 8) - 1, 0), j)
    else:
        halo_map = lambda i, j: (i, j)
    return pl.pallas_call(
        functools.partial(_conv_kernel, blocks_per_seq=geo.blocks_per_seq, zero_first=zero_first),
        out_shape=jax.ShapeDtypeStruct((geo.m, D_MODEL), geo.act),
        grid=(geo.m // r, D_MODEL // tn),
        in_specs=[pl.BlockSpec((r, tn), lambda i, j: (i, j)), pl.BlockSpec((8, tn), halo_map),
                  pl.BlockSpec((r, tn), lambda i, j: (i, j)), pl.BlockSpec((CONV_W, tn), lambda i, j: (0, j))],
        out_specs=pl.BlockSpec((r, tn), lambda i, j: (i, j)),
        compiler_params=_cparams(2, 2 * r * tn * 24),
        name="conv",
    )(u, halo_src, b, conv_w)


def _lambda(lq1_ref, lk1_ref, lq2_ref, lk2_ref, lam_init):
    a = jnp.sum(lq1_ref[...] * lk1_ref[...], axis=-1, keepdims=True)
    b = jnp.sum(lq2_ref[...] * lk2_ref[...], axis=-1, keepdims=True)
    return jnp.exp(a) - jnp.exp(b) + lam_init


def _attn_prompt_kernel(lq1_ref, lk1_ref, lq2_ref, lk2_ref, sub_ref, q_ref, k_ref, v_ref, o_ref, *, tq, nq, lam_init):
    qi = pl.program_id(2)
    lam = _lambda(lq1_ref, lk1_ref, lq2_ref, lk2_ref, lam_init)
    r = lax.broadcasted_iota(jnp.int32, (tq, tq), 0)
    c = lax.broadcasted_iota(jnp.int32, (tq, tq), 1)
    causal = c <= r

    def tile(n0):
        q = q_ref[...]
        outs = []
        for j in range(2):
            cols = slice(j * HEAD_DIM, (j + 1) * HEAD_DIM)
            qj = q[:, cols]
            s_d = jnp.where(causal, _dot_nt(qj, k_ref[n0:n0 + tq, cols]) * SCALE, NEG_INF)
            m = jnp.max(s_d, axis=-1, keepdims=True)
            if n0:
                s_p = _dot_nt(qj, k_ref[0:n0, cols]) * SCALE
                m = jnp.maximum(m, jnp.max(s_p, axis=-1, keepdims=True))
            p_d = jnp.exp(s_d - m)
            l = jnp.sum(p_d, axis=-1, keepdims=True)
            acc = _dot(p_d, v_ref[n0:n0 + tq, :])
            if n0:
                p_p = jnp.exp(s_p - m)
                l = l + jnp.sum(p_p, axis=-1, keepdims=True)
                acc = acc + _dot(p_p, v_ref[0:n0, :])
            outs.append(acc / l)
        o = outs[0] - lam * outs[1]
        o = _rms(o, SUBLN_EPS) * sub_ref[...] * (1.0 - lam_init)
        o_ref[...] = o.astype(o_ref.dtype)

    for i in range(nq):
        pl.when(qi == i)(functools.partial(tile, i * tq))


def _attn_prompt(q, k, v, lam_vecs, subln, lam_init, n_seq, seq_len):
    tq = 256
    nq = seq_len // tq
    vec = pl.BlockSpec((1, HEAD_DIM), lambda b, h, i: (0, 0))
    kv_spec = pl.BlockSpec((seq_len, V_DIM), lambda b, h, i: (b, h))
    q_spec = pl.BlockSpec((tq, V_DIM), lambda b, h, i: (b * nq + i, h))
    return pl.pallas_call(
        functools.partial(_attn_prompt_kernel, tq=tq, nq=nq, lam_init=lam_init),
        out_shape=jax.ShapeDtypeStruct((n_seq * seq_len, N_HEADS * V_DIM), BF16),
        grid=(n_seq, N_HEADS, nq),
        in_specs=[vec, vec, vec, vec, pl.BlockSpec((1, V_DIM), lambda b, h, i: (0, 0)), q_spec, kv_spec, kv_spec],
        out_specs=q_spec,
        compiler_params=_cparams(3, 4 * seq_len * V_DIM * 4 + 10 * tq * seq_len * 4),
        name="attn_prompt",
    )(*lam_vecs, subln, q, k, v)


SUBLANES = 8
HJ_GROUPS = 2 * N_HEADS // SUBLANES
PAGES_PER_STEP = 2


def _gather_rows(refs, i):
    parts = [ref.reshape(PAGE_SIZE * SUBLANES, HEAD_DIM)[pl.ds(i, PAGE_SIZE, stride=SUBLANES), :] for ref in refs]
    return parts[0] if len(parts) == 1 else jnp.concatenate(parts, axis=0)


def _attn_sample_kernel(pt_ref, lq1_ref, lk1_ref, lq2_ref, lk2_ref, sub_ref, q_ref, kn_ref, vn_ref, *rest,
                        n_steps, pps, lam_init):
    n_kv = HJ_GROUPS * pps
    k_refs = [rest[s * HJ_GROUPS:(s + 1) * HJ_GROUPS] for s in range(pps)]
    v_refs = [rest[n_kv + s * 4:n_kv + (s + 1) * 4] for s in range(pps)]
    o_ref, qbd_ref, m_ref, l_ref, acc_ref = rest[n_kv + 4 * pps:]
    p = pl.program_id(1)
    row = lax.broadcasted_iota(jnp.int32, (SUBLANES, V_DIM), 0)
    col = lax.broadcasted_iota(jnp.int32, (SUBLANES, V_DIM), 1)

    @pl.when(p == 0)
    def _():
        for h in range(N_HEADS):
            qh = q_ref[:, h * V_DIM:(h + 1) * V_DIM]
            lower = pltpu.roll(qh, SAMPLE_LEN, 0)
            top = (row < SAMPLE_LEN) & (col < HEAD_DIM)
            bot = (row >= SAMPLE_LEN) & (col >= HEAD_DIM)
            qbd_ref[h] = jnp.where(top, qh, jnp.where(bot, lower, 0.0))
        m_ref[...] = jnp.full(m_ref.shape, NEG_INF, F32)
        l_ref[...] = jnp.zeros(l_ref.shape, F32)
        acc_ref[...] = jnp.zeros(acc_ref.shape, F32)

    scores = []
    for h in range(N_HEADS):
        g, i0 = (2 * h) // SUBLANES, (2 * h) % SUBLANES
        k0 = _gather_rows([k_refs[s][g] for s in range(pps)], i0)
        k1 = _gather_rows([k_refs[s][g] for s in range(pps)], i0 + 1)
        scores.append(_dot_nt(qbd_ref[h], jnp.concatenate([k0, k1], axis=1)))
    s = jnp.concatenate(scores, axis=0) * SCALE
    m_old = m_ref[...]
    m_new = jnp.maximum(m_old, jnp.max(s, axis=-1, keepdims=True))
    corr = jnp.exp(m_old - m_new)
    pr = jnp.exp(s - m_new)
    l_ref[...] = l_ref[...] * corr + jnp.sum(pr, axis=-1, keepdims=True)
    m_ref[...] = m_new
    pv = []
    for h in range(N_HEADS):
        hg, hi = h // SUBLANES, h % SUBLANES
        v = jnp.concatenate([_gather_rows([v_refs[s][2 * hg + half] for s in range(pps)], hi) for half in range(2)], axis=1)
        pv.append(_dot(pr[h * SUBLANES:(h + 1) * SUBLANES], v))
    acc_ref[...] = acc_ref[...] * corr + jnp.concatenate(pv, axis=0)

    @pl.when(p == n_steps - 1)
    def _():
        lam = _lambda(lq1_ref, lk1_ref, lq2_ref, lk2_ref, lam_init)
        t_q = row[:, 0:1] % SAMPLE_LEN
        for h in range(N_HEADS):
            cols = slice(h * V_DIM, (h + 1) * V_DIM)
            qbd = qbd_ref[h]
            kn = kn_ref[:, cols]
            vn = vn_ref[:, cols]
            s_new = [jnp.where(t <= t_q, jnp.sum(qbd * kn[t:t + 1, :], axis=-1, keepdims=True) * SCALE, NEG_INF)
                     for t in range(SAMPLE_LEN)]
            rows = slice(h * SUBLANES, (h + 1) * SUBLANES)
            m_old = m_ref[rows, :]
            m_new = m_old
            for s_t in s_new:
                m_new = jnp.maximum(m_new, s_t)
            corr = jnp.exp(m_old - m_new)
            l = l_ref[rows, :] * corr
            acc = acc_ref[rows, :] * corr
            for t, s_t in enumerate(s_new):
                p_t = jnp.exp(s_t - m_new)
                l = l + p_t
                acc = acc + p_t * vn[t:t + 1, :]
            o = acc / l
            d = o - lam * pltpu.roll(o, SAMPLE_LEN, 0)
            o_ref[:, cols] = _rms(d, SUBLN_EPS) * sub_ref[...] * (1.0 - lam_init)


def _attn_sample(q, k_new, v_new, cache_k3, cache_v3, page0, page_table, lam_vecs, subln, lam_init):
    n_seq, n_pages = page_table.shape
    pps = PAGES_PER_STEP
    n_steps = n_pages // pps
    vec = pl.BlockSpec((1, HEAD_DIM), lambda b, p, pt: (0, 0))
    row_spec = pl.BlockSpec((SAMPLE_PAD, D_MODEL), lambda b, p, pt: (b, 0))
    tile_block = (PAGE_SIZE, SUBLANES, HEAD_DIM)
    k_specs = [pl.BlockSpec(tile_block, (lambda b, p, pt, s=s, g=g: (page0 + pt[b, p * pps + s], g, 0)))
               for s in range(pps) for g in range(HJ_GROUPS)]
    v_specs = [pl.BlockSpec(tile_block, (lambda b, p, pt, s=s, g=g, c=c: (page0 + pt[b, p * pps + s], g, c)))
               for s in range(pps) for g in range(2) for c in range(2)]
    grid_spec = pltpu.PrefetchScalarGridSpec(
        num_scalar_prefetch=1,
        grid=(n_seq, n_steps),
        in_specs=[vec, vec, vec, vec, pl.BlockSpec((1, V_DIM), lambda b, p, pt: (0, 0)), row_spec, row_spec, row_spec]
        + k_specs + v_specs,
        out_specs=row_spec,
        scratch_shapes=[pltpu.VMEM((N_HEADS, SUBLANES, V_DIM), F32), pltpu.VMEM((N_HEADS * SUBLANES, 1), F32),
                        pltpu.VMEM((N_HEADS * SUBLANES, 1), F32), pltpu.VMEM((N_HEADS * SUBLANES, V_DIM), F32)],
    )
    n_blocks = (HJ_GROUPS + 4) * pps
    return pl.pallas_call(
        functools.partial(_attn_sample_kernel, n_steps=n_steps, pps=pps, lam_init=lam_init),
        out_shape=jax.ShapeDtypeStruct((n_seq * SAMPLE_PAD, D_MODEL), F32),
        grid_spec=grid_spec,
        compiler_params=_cparams(2, 2 * n_blocks * PAGE_SIZE * SUBLANES * HEAD_DIM * 4 + 8 * 2**20),
        name="attn_sample",
    )(page_table, *lam_vecs, subln, q, k_new, v_new, *([cache_k3] * (HJ_GROUPS * pps)), *([cache_v3] * (4 * pps)))


def _run_path(geo, tm, x, mods, p, mixer_conv, mixer_attn):
    norm_pre, norm_post = p["norm_pre"], p["norm_post"]
    h = _prenorm(geo, x, mods[0], norm_pre, 0, 0)
    aux = {}
    for l in range(DEPTH):
        for sub in range(N_SUB):
            g_row = l * N_SUB + sub
            if sub == 1:
                if l % 2 == 0:
                    y, aux[l] = mixer_conv(l // 2, h)
                else:
                    y, aux[l] = mixer_attn(l // 2, l, h)
                coef = 1.0
            else:
                a = _mm_swiglu(h, p["ffn_w_in"], (l, sub // 2), tm)
                y = _mm(a, p["ffn_w_out"], (l, sub // 2), 0, D_MODEL, tm, 256, F32, "ffn_out", single_buffer_x=tm >= 1024)
                coef = 0.5
            if sub + 1 < N_SUB:
                nxt, mod_next = (sub + 1, g_row + 1), mods[l]
            elif l + 1 < DEPTH:
                nxt, mod_next = (0, g_row + 1), mods[l + 1]
            else:
                nxt, mod_next = None, None
            x, h = _post(geo, x, y, mods[l], norm_post, sub, g_row, coef, nxt, mod_next, norm_pre)
    return x, aux


def kernel(x_prompt, x_sample, state_conv, cache_k, cache_v, page_table, c_prompt, c_sample,
           ada_w, ada_b, norm_pre, norm_post, ffn_w_in, ffn_w_out,
           conv_w_in, conv_w, conv_w_out, attn_w_qkv, attn_w_out,
           lambda_q1, lambda_k1, lambda_q2, lambda_k2, subln_w):
    p = dict(norm_pre=norm_pre.reshape(DEPTH * N_SUB, D_MODEL), norm_post=norm_post.reshape(DEPTH * N_SUB, D_MODEL),
             ffn_w_in=ffn_w_in, ffn_w_out=ffn_w_out)
    c_all = jnp.zeros((N_SEQ_ALL, D_MODEL), F32).at[:N_PROMPT_SEQ].set(c_prompt)
    c_all = c_all.at[N_PROMPT_SEQ:N_PROMPT_SEQ + N_SAMPLE_SEQ].set(c_sample)
    ada_b3 = ada_b.reshape(DEPTH, 1, N_SUB * N_MOD * D_MODEL)
    mods = [_ada(c_all, ada_w, ada_b3, l) for l in range(DEPTH)]

    m_p = N_PROMPT_SEQ * PROMPT_LEN
    m_s = N_SAMPLE_SEQ * SAMPLE_PAD
    geo_p = _Rows(m_p, 256, PROMPT_LEN, 0, BF16)
    geo_s = _Rows(m_s, SAMPLE_PAD, SAMPLE_PAD, N_PROMPT_SEQ, F32)
    tm_p, tm_s = 1024, m_s
    n_kv = cache_k.shape[1]
    cache_k3 = cache_k.reshape(-1, 2 * N_HEADS, HEAD_DIM)
    cache_v3 = cache_v.reshape(-1, N_HEADS, V_DIM)

    def lam_vecs(i):
        return [a[i].reshape(1, HEAD_DIM) for a in (lambda_q1, lambda_k1, lambda_q2, lambda_k2)]

    def conv_mixer(geo, tm, prompt):
        def run(i, h):
            b, u = _mm_conv_in(h, conv_w_in, (i,), tm)
            if prompt:
                halo_src = u
                new_state = u.reshape(N_PROMPT_SEQ, PROMPT_LEN, D_MODEL)[:, PROMPT_LEN - (CONV_W - 1):]
            else:
                halo_src = jnp.pad(state_conv[i], ((0, 0), (SAMPLE_PAD - (CONV_W - 1), 0), (0, 0))).reshape(m_s, D_MODEL)
                new_state = u.reshape(N_SAMPLE_SEQ, SAMPLE_PAD, D_MODEL)[:, SAMPLE_LEN - (CONV_W - 1):SAMPLE_LEN]
            z = _conv(geo, u, halo_src, b, conv_w[i], zero_first=prompt)
            y = _mm(z, conv_w_out, (i,), 0, D_MODEL, tm, 512, F32, "conv_out")
            return y, new_state
        return run

    def attn_mixer(geo, tm, prompt):
        def run(i, l, h):
            lam_init = 0.8 - 0.6 * math.exp(-0.3 * l)
            q = _mm(h, attn_w_qkv, (i,), 0, D_MODEL, tm, 512, F32, "attn_q")
            k = _mm(h, attn_w_qkv, (i,), D_MODEL, D_MODEL, tm, 512, F32, "attn_k")
            v = _mm(h, attn_w_qkv, (i,), 2 * D_MODEL, D_MODEL, tm, 512, F32, "attn_v")
            if prompt:
                o = _attn_prompt(q, k, v, lam_vecs(i), subln_w[i].reshape(1, V_DIM), lam_init, N_PROMPT_SEQ, PROMPT_LEN)
                kv = (k.reshape(N_PROMPT_SEQ, PROMPT_LEN, N_HEADS, 2, HEAD_DIM),
                      v.reshape(N_PROMPT_SEQ, PROMPT_LEN, N_HEADS, V_DIM))
            else:
                o = _attn_sample(q, k, v, cache_k3, cache_v3, i * n_kv, page_table, lam_vecs(i),
                                 subln_w[i].reshape(1, V_DIM), lam_init)
                kv = (k.reshape(N_SAMPLE_SEQ, SAMPLE_PAD, N_HEADS, 2, HEAD_DIM)[:, :SAMPLE_LEN],
                      v.reshape(N_SAMPLE_SEQ, SAMPLE_PAD, N_HEADS, V_DIM)[:, :SAMPLE_LEN])
            y = _mm(o, attn_w_out, (i,), 0, D_MODEL, tm, 512, F32, "attn_out")
            return y, kv
        return run

    xp = x_prompt.reshape(m_p, D_MODEL)
    xs = jnp.pad(x_sample, ((0, 0), (0, SAMPLE_PAD - SAMPLE_LEN), (0, 0))).reshape(m_s, D_MODEL)
    yp, aux_p = _run_path(geo_p, tm_p, xp, mods, p, conv_mixer(geo_p, tm_p, True), attn_mixer(geo_p, tm_p, True))
    ys, aux_s = _run_path(geo_s, tm_s, xs, mods, p, conv_mixer(geo_s, tm_s, False), attn_mixer(geo_s, tm_s, False))

    conv_layers = [l for l in range(DEPTH) if l % 2 == 0]
    attn_layers = [l for l in range(DEPTH) if l % 2 == 1]
    y_prompt = yp.reshape(N_PROMPT_SEQ, PROMPT_LEN, D_MODEL)
    y_sample = ys.reshape(N_SAMPLE_SEQ, SAMPLE_PAD, D_MODEL)[:, :SAMPLE_LEN]
    return (y_prompt, y_sample,
            jnp.stack([aux_p[l] for l in conv_layers]), jnp.stack([aux_s[l] for l in conv_layers]),
            jnp.stack([aux_p[l][0] for l in attn_layers]), jnp.stack([aux_p[l][1] for l in attn_layers]),
            jnp.stack([aux_s[l][0] for l in attn_layers]), jnp.stack([aux_s[l][1] for l in attn_layers]))
```

```python
import functools
import math

import jax
import jax.numpy as jnp
from jax import lax
from jax.experimental import pallas as pl
from jax.experimental.pallas import tpu as pltpu

F32 = jnp.float32
BF16 = jnp.bfloat16

D_MODEL = 4096
N_PROMPT_SEQ = 4
PROMPT_LEN = 2048
DEPTH = 2
N_SAMPLE_SEQ = 8
SAMPLE_LEN = 4
SAMPLE_PAD = 8
PAGE_SIZE = 128
N_PAGES = 16384 // PAGE_SIZE
D_FF = 11008
CONV_W = 3
N_HEADS = 16
HEAD_DIM = 128
V_DIM = 2 * HEAD_DIM
SCALE = HEAD_DIM ** -0.5
RMS_EPS = 1e-6
SUBLN_EPS = 1e-5
NEG_INF = -1e30
N_SUB = 3
N_MOD = 3
N_SEQ_ALL = 16

V7X_VMEM_BYTES = 64 * 2**20
VMEM_CAP = V7X_VMEM_BYTES - 6 * 2**20
FF_TILE = 256


def _cparams(n_grid, vmem_bytes):
    limit = int(min(VMEM_CAP, max(32 * 2**20, vmem_bytes + 8 * 2**20)))
    return pltpu.CompilerParams(dimension_semantics=("arbitrary",) * n_grid, vmem_limit_bytes=limit)


def _dot(a, b):
    return lax.dot_general(a, b, (((1,), (0,)), ((), ())), preferred_element_type=F32)


def _dot_nt(a, b):
    return lax.dot_general(a, b, (((1,), (1,)), ((), ())), preferred_element_type=F32)


def _sigmoid(x):
    return 1.0 / (1.0 + jnp.exp(-x))


def _rms(x, eps):
    return x * lax.rsqrt(jnp.mean(x * x, axis=-1, keepdims=True) + eps)


def _ada_kernel(c_ref, w_ref, b_ref, o_ref):
    c = c_ref[...]
    o_ref[...] = _dot(c * _sigmoid(c), w_ref[...]) + b_ref[...]


def _ada(c_all, ada_w, ada_b3, layer):
    n = N_SUB * N_MOD * D_MODEL
    tn = 512
    out = pl.pallas_call(
        _ada_kernel,
        out_shape=jax.ShapeDtypeStruct((N_SEQ_ALL, n), F32),
        grid=(n // tn,),
        in_specs=[
            pl.BlockSpec((N_SEQ_ALL, D_MODEL), lambda j: (0, 0)),
            pl.BlockSpec((None, D_MODEL, tn), lambda j: (layer, 0, j)),
            pl.BlockSpec((None, 1, tn), lambda j: (layer, 0, j)),
        ],
        out_specs=pl.BlockSpec((N_SEQ_ALL, tn), lambda j: (0, j)),
        compiler_params=_cparams(1, 2 * D_MODEL * tn * 4),
        name=f"ada_l{layer}",
    )(c_all, ada_w, ada_b3)
    return out.reshape(N_SEQ_ALL, N_SUB * N_MOD, D_MODEL)


class _Rows:
    def __init__(self, m, rows, rows_per_seq, seq0, act):
        self.m, self.rows, self.blocks_per_seq, self.seq0 = m, rows, rows_per_seq // rows, seq0
        self.act = act

    def seq(self, i):
        return i // self.blocks_per_seq + self.seq0


def _modulate(x, mod_ref, sub, g_ref, g_row):
    shift = mod_ref[N_MOD * sub:N_MOD * sub + 1, :]
    scale = mod_ref[N_MOD * sub + 1:N_MOD * sub + 2, :]
    return _rms(x, RMS_EPS) * g_ref[g_row:g_row + 1, :] * (1.0 + scale) + shift


def _prenorm_kernel(x_ref, mod_ref, g_ref, h_ref, *, sub, g_row):
    h_ref[...] = _modulate(x_ref[...], mod_ref, sub, g_ref, g_row).astype(h_ref.dtype)


def _prenorm(geo, x, mod, g_pre, sub, g_row):
    r = geo.rows
    return pl.pallas_call(
        functools.partial(_prenorm_kernel, sub=sub, g_row=g_row),
        out_shape=jax.ShapeDtypeStruct((geo.m, D_MODEL), geo.act),
        grid=(geo.m // r,),
        in_specs=[
            pl.BlockSpec((r, D_MODEL), lambda i: (i, 0)),
            pl.BlockSpec((None, N_SUB * N_MOD, D_MODEL), lambda i: (geo.seq(i), 0, 0)),
            pl.BlockSpec(g_pre.shape, lambda i: (0, 0)),
        ],
        out_specs=pl.BlockSpec((r, D_MODEL), lambda i: (i, 0)),
        compiler_params=_cparams(1, 2 * r * D_MODEL * 6),
        name="prenorm",
    )(x, mod, g_pre)


def _post_kernel(x_ref, y_ref, mod_ref, gpost_ref, *rest, sub, g_row, coef, nxt):
    gate = mod_ref[N_MOD * sub + 2:N_MOD * sub + 3, :]
    y = _rms(y_ref[...], RMS_EPS) * gpost_ref[g_row:g_row + 1, :]
    x_new = x_ref[...] + (coef * gate) * y
    if nxt is None:
        (xo_ref,) = rest
        xo_ref[...] = x_new
    else:
        modn_ref, gpre_ref, xo_ref, h_ref = rest
        xo_ref[...] = x_new
        h_ref[...] = _modulate(x_new, modn_ref, nxt[0], gpre_ref, nxt[1]).astype(h_ref.dtype)


def _post(geo, x, y, mod, g_post, sub, g_row, coef, nxt=None, mod_next=None, g_pre=None):
    r = geo.rows
    row_spec = pl.BlockSpec((r, D_MODEL), lambda i: (i, 0))
    mod_spec = pl.BlockSpec((None, N_SUB * N_MOD, D_MODEL), lambda i: (geo.seq(i), 0, 0))
    in_specs = [row_spec, row_spec, mod_spec, pl.BlockSpec(g_post.shape, lambda i: (0, 0))]
    args = [x, y, mod, g_post]
    out_shape = [jax.ShapeDtypeStruct((geo.m, D_MODEL), F32)]
    out_specs = [row_spec]
    if nxt is not None:
        in_specs += [mod_spec, pl.BlockSpec(g_pre.shape, lambda i: (0, 0))]
        args += [mod_next, g_pre]
        out_shape.append(jax.ShapeDtypeStruct((geo.m, D_MODEL), geo.act))
        out_specs.append(row_spec)
    res = pl.pallas_call(
        functools.partial(_post_kernel, sub=sub, g_row=g_row, coef=coef, nxt=nxt),
        out_shape=out_shape,
        grid=(geo.m // r,),
        in_specs=in_specs,
        out_specs=out_specs,
        compiler_params=_cparams(1, 2 * r * D_MODEL * 14),
        name="post",
    )(*args)
    return (res[0], res[1]) if nxt is not None else (res[0], None)


def _w_spec(w, lead, k, tn, col_block0):
    none = (None,) * len(lead)
    return pl.BlockSpec(none + (k, tn), lambda i, j: lead + (0, j + col_block0))


def _dual_specs(xp, xs, tm, tn, nj, single_buffer_x=False):
    k = xp.shape[1]
    x_kw = dict(pipeline_mode=pl.Buffered(1)) if single_buffer_x else {}
    ms = xs.shape[0]
    in_specs = [pl.BlockSpec((tm, k), lambda i, j: (i, 0), **x_kw), pl.BlockSpec((ms, k), lambda i, j: (0, 0))]
    out_specs = [pl.BlockSpec((tm, tn), lambda i, j: (i, j)),
                 pl.BlockSpec((ms, tn), lambda i, j: (0, jnp.where(i == 0, j, nj - 1)))]
    return in_specs, out_specs


def _first_sweep(fn):
    pl.when(pl.program_id(0) == 0)(fn)


def _mm_kernel(xp_ref, xs_ref, w_ref, op_ref, os_ref):
    w = w_ref[...]
    op_ref[...] = _dot(xp_ref[...], w).astype(op_ref.dtype)

    def sample():
        os_ref[...] = _dot(xs_ref[...], w).astype(os_ref.dtype)
    _first_sweep(sample)


def _mm(x2, w, lead, col0, n, tm, tn, out_dtype, name, single_buffer_x=False):
    xp, xs = x2
    (m, k), ms = xp.shape, xs.shape[0]
    nj = n // tn
    in_specs, out_specs = _dual_specs(xp, xs, tm, tn, nj, single_buffer_x)
    x_bytes = tm * k * xp.dtype.itemsize * (1 if single_buffer_x else 2) + 2 * ms * k * 4
    vmem = x_bytes + 2 * k * tn * 4 + 2 * (tm + ms) * tn * 4
    return pl.pallas_call(
        _mm_kernel,
        out_shape=[jax.ShapeDtypeStruct((m, n), out_dtype), jax.ShapeDtypeStruct((ms, n), out_dtype)],
        grid=(m // tm, nj),
        in_specs=in_specs + [_w_spec(w, lead, k, tn, col0 // tn)],
        out_specs=out_specs,
        compiler_params=_cparams(2, vmem),
        name=name,
    )(xp, xs, w)


def _swiglu_kernel(hp_ref, hs_ref, wg_ref, wu_ref, ap_ref, as_ref):
    wg, wu = wg_ref[...], wu_ref[...]

    def swiglu(h_ref, a_ref):
        h = h_ref[...]
        g = _dot(h, wg)
        a_ref[...] = (g * _sigmoid(g) * _dot(h, wu)).astype(a_ref.dtype)
    swiglu(hp_ref, ap_ref)
    _first_sweep(functools.partial(swiglu, hs_ref, as_ref))


def _mm_swiglu(h2, w_in, lead, tm):
    hp, hs = h2
    (m, k), ms = hp.shape, hs.shape[0]
    tn = FF_TILE
    nj = D_FF // tn
    in_specs, out_specs = _dual_specs(hp, hs, tm, tn, nj)
    vmem = 2 * tm * k * 2 + 2 * ms * k * 4 + 4 * k * tn * 4 + 2 * (tm + ms) * tn * 2 + 3 * tm * tn * 4
    return pl.pallas_call(
        _swiglu_kernel,
        out_shape=[jax.ShapeDtypeStruct((m, D_FF), BF16), jax.ShapeDtypeStruct((ms, D_FF), BF16)],
        grid=(m // tm, nj),
        in_specs=in_specs + [_w_spec(w_in, lead, k, tn, 0), _w_spec(w_in, lead, k, tn, nj)],
        out_specs=out_specs,
        compiler_params=_cparams(2, vmem),
        name="ffn_in",
    )(hp, hs, w_in, w_in)


def _conv_in_kernel(hp_ref, hs_ref, wb_ref, wc_ref, wx_ref, bp_ref, up_ref, bs_ref, us_ref):
    wb, wc, wx = wb_ref[...], wc_ref[...], wx_ref[...]

    def gates(h_ref, b_ref, u_ref):
        h = h_ref[...]
        b_ref[...] = _dot(h, wb)
        u_ref[...] = _dot(h, wc) * _dot(h, wx)
    gates(hp_ref, bp_ref, up_ref)
    _first_sweep(functools.partial(gates, hs_ref, bs_ref, us_ref))


def _mm_conv_in(h2, w_in, lead, tm):
    hp, hs = h2
    (m, k), ms = hp.shape, hs.shape[0]
    tn = 256
    nb = D_MODEL // tn
    in_specs, (op_spec, os_spec) = _dual_specs(hp, hs, tm, tn, nb)
    vmem = 2 * tm * k * 2 + 2 * ms * k * 4 + 6 * k * tn * 4 + 4 * (tm + ms) * tn * 4 + 3 * tm * tn * 4
    out_p = jax.ShapeDtypeStruct((m, D_MODEL), F32)
    out_s = jax.ShapeDtypeStruct((ms, D_MODEL), F32)
    bp, up, bs, us = pl.pallas_call(
        _conv_in_kernel,
        out_shape=[out_p, out_p, out_s, out_s],
        grid=(m // tm, nb),
        in_specs=in_specs + [_w_spec(w_in, lead, k, tn, 0), _w_spec(w_in, lead, k, tn, nb), _w_spec(w_in, lead, k, tn, 2 * nb)],
        out_specs=[op_spec, op_spec, os_spec, os_spec],
        compiler_params=_cparams(2, vmem),
        name="conv_in",
    )(hp, hs, w_in, w_in, w_in)
    return (bp, bs), (up, us)


def _conv_kernel(u_ref, halo_ref, b_ref, cw_ref, z_ref, *, blocks_per_seq, zero_first):
    u = u_ref[...]
    halo = halo_ref[...]
    if zero_first:
        first = (pl.program_id(0) % blocks_per_seq) == 0
        halo = jnp.where(first, 0.0, halo)
    row = lax.broadcasted_iota(jnp.int32, u.shape, 0)
    prev1 = halo[7:8, :]
    prev2 = halo[6:7, :]
    u1 = jnp.where(row == 0, prev1, pltpu.roll(u, 1, 0))
    u2 = jnp.where(row == 0, prev2, jnp.where(row == 1, prev1, pltpu.roll(u, 2, 0)))
    conv = cw_ref[0:1, :] * u2 + cw_ref[1:2, :] * u1 + cw_ref[2:3, :] * u
    z_ref[...] = (b_ref[...] * conv).astype(z_ref.dtype)


def _conv(geo, u, halo_src, b, conv_w, zero_first):
    r = geo.rows
    tn = 1024
    if zero_first:
        halo_map = lambda i, j: (jnp.maximum(i * (r // 8) - 1, 0), j)
    else:
        halo_map = lambda i, j: (i, j)
    return pl.pallas_call(
        functools.partial(_conv_kernel, blocks_per_seq=geo.blocks_per_seq, zero_first=zero_first),
        out_shape=jax.ShapeDtypeStruct((geo.m, D_MODEL), geo.act),
        grid=(geo.m // r, D_MODEL // tn),
        in_specs=[pl.BlockSpec((r, tn), lambda i, j: (i, j)), pl.BlockSpec((8, tn), halo_map),
                  pl.BlockSpec((r, tn), lambda i, j: (i, j)), pl.BlockSpec((CONV_W, tn), lambda i, j: (0, j))],
        out_specs=pl.BlockSpec((r, tn), lambda i, j: (i, j)),
        compiler_params=_cparams(2, 2 * r * tn * 24),
        name="conv",
    )(u, halo_src, b, conv_w)


def _lambda(lq1_ref, lk1_ref, lq2_ref, lk2_ref, lam_init):
    a = jnp.sum(lq1_ref[...] * lk1_ref[...], axis=-1, keepdims=True)
    b = jnp.sum(lq2_ref[...] * lk2_ref[...], axis=-1, keepdims=True)
    return jnp.exp(a) - jnp.exp(b) + lam_init


def _attn_prompt_kernel(lq1_ref, lk1_ref, lq2_ref, lk2_ref, sub_ref, q_ref, k_ref, v_ref, o_ref, *, tq, nq, lam_init):
    qi = pl.program_id(2)
    lam = _lambda(lq1_ref, lk1_ref, lq2_ref, lk2_ref, lam_init)
    r = lax.broadcasted_iota(jnp.int32, (tq, tq), 0)
    c = lax.broadcasted_iota(jnp.int32, (tq, tq), 1)
    causal = c <= r

    def tile(n0):
        q = q_ref[...]
        outs = []
        for j in range(2):
            cols = slice(j * HEAD_DIM, (j + 1) * HEAD_DIM)
            qj = q[:, cols]
            s_d = jnp.where(causal, _dot_nt(qj, k_ref[n0:n0 + tq, cols]) * SCALE, NEG_INF)
            m = jnp.max(s_d, axis=-1, keepdims=True)
            if n0:
                s_p = _dot_nt(qj, k_ref[0:n0, cols]) * SCALE
                m = jnp.maximum(m, jnp.max(s_p, axis=-1, keepdims=True))
            p_d = jnp.exp(s_d - m)
            l = jnp.sum(p_d, axis=-1, keepdims=True)
            acc = _dot(p_d, v_ref[n0:n0 + tq, :])
            if n0:
                p_p = jnp.exp(s_p - m)
                l = l + jnp.sum(p_p, axis=-1, keepdims=True)
                acc = acc + _dot(p_p, v_ref[0:n0, :])
            outs.append(acc / l)
        o = outs[0] - lam * outs[1]
        o = _rms(o, SUBLN_EPS) * sub_ref[...] * (1.0 - lam_init)
        o_ref[...] = o.astype(o_ref.dtype)

    for i in range(nq):
        pl.when(qi == i)(functools.partial(tile, i * tq))


def _attn_prompt(q, k, v, lam_vecs, subln, lam_init, n_seq, seq_len):
    tq = 256
    nq = seq_len // tq
    vec = pl.BlockSpec((1, HEAD_DIM), lambda b, h, i: (0, 0))
    kv_spec = pl.BlockSpec((seq_len, V_DIM), lambda b, h, i: (b, h))
    q_spec = pl.BlockSpec((tq, V_DIM), lambda b, h, i: (b * nq + i, h))
    return pl.pallas_call(
        functools.partial(_attn_prompt_kernel, tq=tq, nq=nq, lam_init=lam_init),
        out_shape=jax.ShapeDtypeStruct((n_seq * seq_len, N_HEADS * V_DIM), BF16),
        grid=(n_seq, N_HEADS, nq),
        in_specs=[vec, vec, vec, vec, pl.BlockSpec((1, V_DIM), lambda b, h, i: (0, 0)), q_spec, kv_spec, kv_spec],
        out_specs=q_spec,
        compiler_params=_cparams(3, 4 * seq_len * V_DIM * 4 + 10 * tq * seq_len * 4),
        name="attn_prompt",
    )(*lam_vecs, subln, q, k, v)


SUBLANES = 8
HJ_GROUPS = 2 * N_HEADS // SUBLANES
PAGES_PER_STEP = 2


def _gather_rows(refs, i):
    parts = [ref.reshape(PAGE_SIZE * SUBLANES, HEAD_DIM)[pl.ds(i, PAGE_SIZE, stride=SUBLANES), :] for ref in refs]
    return parts[0] if len(parts) == 1 else jnp.concatenate(parts, axis=0)


def _attn_sample_kernel(pt_ref, lq1_ref, lk1_ref, lq2_ref, lk2_ref, sub_ref, q_ref, kn_ref, vn_ref, *rest,
                        n_steps, pps, lam_init):
    n_kv = HJ_GROUPS * pps
    k_refs = [rest[s * HJ_GROUPS:(s + 1) * HJ_GROUPS] for s in range(pps)]
    v_refs = [rest[n_kv + s * 4:n_kv + (s + 1) * 4] for s in range(pps)]
    o_ref, qbd_ref, m_ref, l_ref, acc_ref = rest[n_kv + 4 * pps:]
    p = pl.program_id(1)
    row = lax.broadcasted_iota(jnp.int32, (SUBLANES, V_DIM), 0)
    col = lax.broadcasted_iota(jnp.int32, (SUBLANES, V_DIM), 1)

    @pl.when(p == 0)
    def _():
        for h in range(N_HEADS):
            qh = q_ref[:, h * V_DIM:(h + 1) * V_DIM]
            lower = pltpu.roll(qh, SAMPLE_LEN, 0)
            top = (row < SAMPLE_LEN) & (col < HEAD_DIM)
            bot = (row >= SAMPLE_LEN) & (col >= HEAD_DIM)
            qbd_ref[h] = jnp.where(top, qh, jnp.where(bot, lower, 0.0))
        m_ref[...] = jnp.full(m_ref.shape, NEG_INF, F32)
        l_ref[...] = jnp.zeros(l_ref.shape, F32)
        acc_ref[...] = jnp.zeros(acc_ref.shape, F32)

    scores = []
    for h in range(N_HEADS):
        g, i0 = (2 * h) // SUBLANES, (2 * h) % SUBLANES
        k0 = _gather_rows([k_refs[s][g] for s in range(pps)], i0)
        k1 = _gather_rows([k_refs[s][g] for s in range(pps)], i0 + 1)
        scores.append(_dot_nt(qbd_ref[h], jnp.concatenate([k0, k1], axis=1)))
    s = jnp.concatenate(scores, axis=0) * SCALE
    m_old = m_ref[...]
    m_new = jnp.maximum(m_old, jnp.max(s, axis=-1, keepdims=True))
    corr = jnp.exp(m_old - m_new)
    pr = jnp.exp(s - m_new)
    l_ref[...] = l_ref[...] * corr + jnp.sum(pr, axis=-1, keepdims=True)
    m_ref[...] = m_new
    pv = []
    for h in range(N_HEADS):
        hg, hi = h // SUBLANES, h % SUBLANES
        v = jnp.concatenate([_gather_rows([v_refs[s][2 * hg + half] for s in range(pps)], hi) for half in range(2)], axis=1)
        pv.append(_dot(pr[h * SUBLANES:(h + 1) * SUBLANES], v))
    acc_ref[...] = acc_ref[...] * corr + jnp.concatenate(pv, axis=0)

    @pl.when(p == n_steps - 1)
    def _():
        lam = _lambda(lq1_ref, lk1_ref, lq2_ref, lk2_ref, lam_init)
        t_q = row[:, 0:1] % SAMPLE_LEN
        for h in range(N_HEADS):
            cols = slice(h * V_DIM, (h + 1) * V_DIM)
            qbd = qbd_ref[h]
            kn = kn_ref[:, cols]
            vn = vn_ref[:, cols]
            s_new = [jnp.where(t <= t_q, jnp.sum(qbd * kn[t:t + 1, :], axis=-1, keepdims=True) * SCALE, NEG_INF)
                     for t in range(SAMPLE_LEN)]
            rows = slice(h * SUBLANES, (h + 1) * SUBLANES)
            m_old = m_ref[rows, :]
            m_new = m_old
            for s_t in s_new:
                m_new = jnp.maximum(m_new, s_t)
            corr = jnp.exp(m_old - m_new)
            l = l_ref[rows, :] * corr
            acc = acc_ref[rows, :] * corr
            for t, s_t in enumerate(s_new):
                p_t = jnp.exp(s_t - m_new)
                l = l + p_t
                acc = acc + p_t * vn[t:t + 1, :]
            o = acc / l
            d = o - lam * pltpu.roll(o, SAMPLE_LEN, 0)
            o_ref[:, cols] = _rms(d, SUBLN_EPS) * sub_ref[...] * (1.0 - lam_init)


def _attn_sample(q, k_new, v_new, cache_k3, cache_v3, page0, page_table, lam_vecs, subln, lam_init):
    n_seq, n_pages = page_table.shape
    pps = PAGES_PER_STEP
    n_steps = n_pages // pps
    vec = pl.BlockSpec((1, HEAD_DIM), lambda b, p, pt: (0, 0))
    row_spec = pl.BlockSpec((SAMPLE_PAD, D_MODEL), lambda b, p, pt: (b, 0))
    tile_block = (PAGE_SIZE, SUBLANES, HEAD_DIM)
    k_specs = [pl.BlockSpec(tile_block, (lambda b, p, pt, s=s, g=g: (page0 + pt[b, p * pps + s], g, 0)))
               for s in range(pps) for g in range(HJ_GROUPS)]
    v_specs = [pl.BlockSpec(tile_block, (lambda b, p, pt, s=s, g=g, c=c: (page0 + pt[b, p * pps + s], g, c)))
               for s in range(pps) for g in range(2) for c in range(2)]
    grid_spec = pltpu.PrefetchScalarGridSpec(
        num_scalar_prefetch=1,
        grid=(n_seq, n_steps),
        in_specs=[vec, vec, vec, vec, pl.BlockSpec((1, V_DIM), lambda b, p, pt: (0, 0)), row_spec, row_spec, row_spec]
        + k_specs + v_specs,
        out_specs=row_spec,
        scratch_shapes=[pltpu.VMEM((N_HEADS, SUBLANES, V_DIM), F32), pltpu.VMEM((N_HEADS * SUBLANES, 1), F32),
                        pltpu.VMEM((N_HEADS * SUBLANES, 1), F32), pltpu.VMEM((N_HEADS * SUBLANES, V_DIM), F32)],
    )
    n_blocks = (HJ_GROUPS + 4) * pps
    return pl.pallas_call(
        functools.partial(_attn_sample_kernel, n_steps=n_steps, pps=pps, lam_init=lam_init),
        out_shape=jax.ShapeDtypeStruct((n_seq * SAMPLE_PAD, D_MODEL), F32),
        grid_spec=grid_spec,
        compiler_params=_cparams(2, 2 * n_blocks * PAGE_SIZE * SUBLANES * HEAD_DIM * 4 + 8 * 2**20),
        name="attn_sample",
    )(page_table, *lam_vecs, subln, q, k_new, v_new, *([cache_k3] * (HJ_GROUPS * pps)), *([cache_v3] * (4 * pps)))


def _run_layers(geos, tm, x2, mods, p, mixer_conv, mixer_attn):
    norm_pre, norm_post = p["norm_pre"], p["norm_post"]
    h2 = [_prenorm(geo, x, mods[0], norm_pre, 0, 0) for geo, x in zip(geos, x2)]
    aux = {}
    for l in range(DEPTH):
        for sub in range(N_SUB):
            g_row = l * N_SUB + sub
            if sub == 1:
                if l % 2 == 0:
                    y2, aux[l] = mixer_conv(l // 2, h2)
                else:
                    y2, aux[l] = mixer_attn(l // 2, l, h2)
                coef = 1.0
            else:
                a2 = _mm_swiglu(h2, p["ffn_w_in"], (l, sub // 2), tm)
                y2 = _mm(a2, p["ffn_w_out"], (l, sub // 2), 0, D_MODEL, tm, 256, F32, "ffn_out", single_buffer_x=True)
                coef = 0.5
            if sub + 1 < N_SUB:
                nxt, mod_next = (sub + 1, g_row + 1), mods[l]
            elif l + 1 < DEPTH:
                nxt, mod_next = (0, g_row + 1), mods[l + 1]
            else:
                nxt, mod_next = None, None
            res = [_post(geo, x, y, mods[l], norm_post, sub, g_row, coef, nxt, mod_next, norm_pre)
                   for geo, x, y in zip(geos, x2, y2)]
            x2 = [r[0] for r in res]
            h2 = [r[1] for r in res]
    return x2, aux


def kernel(x_prompt, x_sample, state_conv, cache_k, cache_v, page_table, c_prompt, c_sample,
           ada_w, ada_b, norm_pre, norm_post, ffn_w_in, ffn_w_out,
           conv_w_in, conv_w, conv_w_out, attn_w_qkv, attn_w_out,
           lambda_q1, lambda_k1, lambda_q2, lambda_k2, subln_w):
    p = dict(norm_pre=norm_pre.reshape(DEPTH * N_SUB, D_MODEL), norm_post=norm_post.reshape(DEPTH * N_SUB, D_MODEL),
             ffn_w_in=ffn_w_in, ffn_w_out=ffn_w_out)
    c_all = jnp.zeros((N_SEQ_ALL, D_MODEL), F32).at[:N_PROMPT_SEQ].set(c_prompt)
    c_all = c_all.at[N_PROMPT_SEQ:N_PROMPT_SEQ + N_SAMPLE_SEQ].set(c_sample)
    ada_b3 = ada_b.reshape(DEPTH, 1, N_SUB * N_MOD * D_MODEL)
    mods = [_ada(c_all, ada_w, ada_b3, l) for l in range(DEPTH)]

    m_p = N_PROMPT_SEQ * PROMPT_LEN
    m_s = N_SAMPLE_SEQ * SAMPLE_PAD
    geo_p = _Rows(m_p, 256, PROMPT_LEN, 0, BF16)
    geo_s = _Rows(m_s, SAMPLE_PAD, SAMPLE_PAD, N_PROMPT_SEQ, F32)
    geos = (geo_p, geo_s)
    tm = 1024
    n_kv = cache_k.shape[1]
    cache_k3 = cache_k.reshape(-1, 2 * N_HEADS, HEAD_DIM)
    cache_v3 = cache_v.reshape(-1, N_HEADS, V_DIM)

    def lam_vecs(i):
        return [a[i].reshape(1, HEAD_DIM) for a in (lambda_q1, lambda_k1, lambda_q2, lambda_k2)]

    def conv_mixer(i, h2):
        b2, u2 = _mm_conv_in(h2, conv_w_in, (i,), tm)
        up, us = u2
        state_p = up.reshape(N_PROMPT_SEQ, PROMPT_LEN, D_MODEL)[:, PROMPT_LEN - (CONV_W - 1):]
        state_s = us.reshape(N_SAMPLE_SEQ, SAMPLE_PAD, D_MODEL)[:, SAMPLE_LEN - (CONV_W - 1):SAMPLE_LEN]
        halo_s = jnp.pad(state_conv[i], ((0, 0), (SAMPLE_PAD - (CONV_W - 1), 0), (0, 0))).reshape(m_s, D_MODEL)
        z2 = (_conv(geo_p, up, up, b2[0], conv_w[i], zero_first=True),
              _conv(geo_s, us, halo_s, b2[1], conv_w[i], zero_first=False))
        y2 = _mm(z2, conv_w_out, (i,), 0, D_MODEL, tm, 512, F32, "conv_out")
        return y2, (state_p, state_s)

    def attn_mixer(i, l, h2):
        lam_init = 0.8 - 0.6 * math.exp(-0.3 * l)
        qp, qs = _mm(h2, attn_w_qkv, (i,), 0, D_MODEL, tm, 512, F32, "attn_q")
        kp, ks = _mm(h2, attn_w_qkv, (i,), D_MODEL, D_MODEL, tm, 512, F32, "attn_k")
        vp, vs = _mm(h2, attn_w_qkv, (i,), 2 * D_MODEL, D_MODEL, tm, 512, F32, "attn_v")
        sub = subln_w[i].reshape(1, V_DIM)
        o2 = (_attn_prompt(qp, kp, vp, lam_vecs(i), sub, lam_init, N_PROMPT_SEQ, PROMPT_LEN),
              _attn_sample(qs, ks, vs, cache_k3, cache_v3, i * n_kv, page_table, lam_vecs(i), sub, lam_init))
        kv_p = (kp.reshape(N_PROMPT_SEQ, PROMPT_LEN, N_HEADS, 2, HEAD_DIM), vp.reshape(N_PROMPT_SEQ, PROMPT_LEN, N_HEADS, V_DIM))
        kv_s = (ks.reshape(N_SAMPLE_SEQ, SAMPLE_PAD, N_HEADS, 2, HEAD_DIM)[:, :SAMPLE_LEN],
                vs.reshape(N_SAMPLE_SEQ, SAMPLE_PAD, N_HEADS, V_DIM)[:, :SAMPLE_LEN])
        y2 = _mm(o2, attn_w_out, (i,), 0, D_MODEL, tm, 512, F32, "attn_out")
        return y2, (kv_p, kv_s)

    xp = x_prompt.reshape(m_p, D_MODEL)
    xs = jnp.pad(x_sample, ((0, 0), (0, SAMPLE_PAD - SAMPLE_LEN), (0, 0))).reshape(m_s, D_MODEL)
    (yp, ys), aux = _run_layers(geos, tm, (xp, xs), mods, p, conv_mixer, attn_mixer)

    conv_layers = [l for l in range(DEPTH) if l % 2 == 0]
    attn_layers = [l for l in range(DEPTH) if l % 2 == 1]
    y_prompt = yp.reshape(N_PROMPT_SEQ, PROMPT_LEN, D_MODEL)
    y_sample = ys.reshape(N_SAMPLE_SEQ, SAMPLE_PAD, D_MODEL)[:, :SAMPLE_LEN]
    return (y_prompt, y_sample,
            jnp.stack([aux[l][0] for l in conv_layers]), jnp.stack([aux[l][1] for l in conv_layers]),
            jnp.stack([aux[l][0][0] for l in attn_layers]), jnp.stack([aux[l][0][1] for l in attn_layers]),
            jnp.stack([aux[l][1][0] for l in attn_layers]), jnp.stack([aux[l][1][1] for l in attn_layers]))
```

```python
import functools
import math

import jax
import jax.numpy as jnp
from jax import lax
from jax.experimental import pallas as pl
from jax.experimental.pallas import tpu as pltpu

F32 = jnp.float32
BF16 = jnp.bfloat16

D_MODEL = 4096
N_PROMPT_SEQ = 4
PROMPT_LEN = 2048
DEPTH = 2
N_SAMPLE_SEQ = 8
SAMPLE_LEN = 4
SAMPLE_PAD = 8
PAGE_SIZE = 128
N_PAGES = 16384 // PAGE_SIZE
D_FF = 11008
CONV_W = 3
N_HEADS = 16
HEAD_DIM = 128
V_DIM = 2 * HEAD_DIM
SCALE = HEAD_DIM ** -0.5
RMS_EPS = 1e-6
SUBLN_EPS = 1e-5
NEG_INF = -1e30
N_SUB = 3
N_MOD = 3
N_SEQ_ALL = 16

V7X_VMEM_BYTES = 64 * 2**20
VMEM_CAP = V7X_VMEM_BYTES - 6 * 2**20
FF_TILE = 256


def _cparams(n_grid, vmem_bytes):
    limit = int(min(VMEM_CAP, max(32 * 2**20, vmem_bytes + 8 * 2**20)))
    return pltpu.CompilerParams(dimension_semantics=("arbitrary",) * n_grid, vmem_limit_bytes=limit)


def _dot(a, b):
    return lax.dot_general(a, b, (((1,), (0,)), ((), ())), preferred_element_type=F32)


def _dot_nt(a, b):
    return lax.dot_general(a, b, (((1,), (1,)), ((), ())), preferred_element_type=F32)


def _sigmoid(x):
    return 1.0 / (1.0 + jnp.exp(-x))


def _rms(x, eps):
    return x * lax.rsqrt(jnp.mean(x * x, axis=-1, keepdims=True) + eps)


def _ada_kernel(c_ref, w_ref, b_ref, o_ref):
    c = c_ref[...]
    o_ref[...] = _dot(c * _sigmoid(c), w_ref[...]) + b_ref[...]


def _ada(c_all, ada_w, ada_b3, layer):
    n = N_SUB * N_MOD * D_MODEL
    tn = 512
    out = pl.pallas_call(
        _ada_kernel,
        out_shape=jax.ShapeDtypeStruct((N_SEQ_ALL, n), F32),
        grid=(n // tn,),
        in_specs=[
            pl.BlockSpec((N_SEQ_ALL, D_MODEL), lambda j: (0, 0)),
            pl.BlockSpec((None, D_MODEL, tn), lambda j: (layer, 0, j)),
            pl.BlockSpec((None, 1, tn), lambda j: (layer, 0, j)),
        ],
        out_specs=pl.BlockSpec((N_SEQ_ALL, tn), lambda j: (0, j)),
        compiler_params=_cparams(1, 2 * D_MODEL * tn * 4),
        name=f"ada_l{layer}",
    )(c_all, ada_w, ada_b3)
    return out.reshape(N_SEQ_ALL, N_SUB * N_MOD, D_MODEL)


class _Rows:
    def __init__(self, m, rows, rows_per_seq, seq0, act):
        self.m, self.rows, self.blocks_per_seq, self.seq0 = m, rows, rows_per_seq // rows, seq0
        self.act = act

    def seq(self, i):
        return i // self.blocks_per_seq + self.seq0


def _modulate(x, mod_ref, sub, g_ref, g_row):
    shift = mod_ref[N_MOD * sub:N_MOD * sub + 1, :]
    scale = mod_ref[N_MOD * sub + 1:N_MOD * sub + 2, :]
    return _rms(x, RMS_EPS) * g_ref[g_row:g_row + 1, :] * (1.0 + scale) + shift


def _prenorm_kernel(x_ref, mod_ref, g_ref, h_ref, *, sub, g_row):
    h_ref[...] = _modulate(x_ref[...], mod_ref, sub, g_ref, g_row).astype(h_ref.dtype)


def _prenorm(geo, x, mod, g_pre, sub, g_row):
    r = geo.rows
    return pl.pallas_call(
        functools.partial(_prenorm_kernel, sub=sub, g_row=g_row),
        out_shape=jax.ShapeDtypeStruct((geo.m, D_MODEL), geo.act),
        grid=(geo.m // r,),
        in_specs=[
            pl.BlockSpec((r, D_MODEL), lambda i: (i, 0)),
            pl.BlockSpec((None, N_SUB * N_MOD, D_MODEL), lambda i: (geo.seq(i), 0, 0)),
            pl.BlockSpec(g_pre.shape, lambda i: (0, 0)),
        ],
        out_specs=pl.BlockSpec((r, D_MODEL), lambda i: (i, 0)),
        compiler_params=_cparams(1, 2 * r * D_MODEL * 6),
        name="prenorm",
    )(x, mod, g_pre)


def _post_kernel(x_ref, y_ref, mod_ref, gpost_ref, *rest, sub, g_row, coef, nxt):
    gate = mod_ref[N_MOD * sub + 2:N_MOD * sub + 3, :]
    y = _rms(y_ref[...], RMS_EPS) * gpost_ref[g_row:g_row + 1, :]
    x_new = x_ref[...] + (coef * gate) * y
    if nxt is None:
        (xo_ref,) = rest
        xo_ref[...] = x_new
    else:
        modn_ref, gpre_ref, xo_ref, h_ref = rest
        xo_ref[...] = x_new
        h_ref[...] = _modulate(x_new, modn_ref, nxt[0], gpre_ref, nxt[1]).astype(h_ref.dtype)


def _post(geo, x, y, mod, g_post, sub, g_row, coef, nxt=None, mod_next=None, g_pre=None):
    r = geo.rows
    row_spec = pl.BlockSpec((r, D_MODEL), lambda i: (i, 0))
    mod_spec = pl.BlockSpec((None, N_SUB * N_MOD, D_MODEL), lambda i: (geo.seq(i), 0, 0))
    in_specs = [row_spec, row_spec, mod_spec, pl.BlockSpec(g_post.shape, lambda i: (0, 0))]
    args = [x, y, mod, g_post]
    out_shape = [jax.ShapeDtypeStruct((geo.m, D_MODEL), F32)]
    out_specs = [row_spec]
    if nxt is not None:
        in_specs += [mod_spec, pl.BlockSpec(g_pre.shape, lambda i: (0, 0))]
        args += [mod_next, g_pre]
        out_shape.append(jax.ShapeDtypeStruct((geo.m, D_MODEL), geo.act))
        out_specs.append(row_spec)
    res = pl.pallas_call(
        functools.partial(_post_kernel, sub=sub, g_row=g_row, coef=coef, nxt=nxt),
        out_shape=out_shape,
        grid=(geo.m // r,),
        in_specs=in_specs,
        out_specs=out_specs,
        compiler_params=_cparams(1, 2 * r * D_MODEL * 14),
        name="post",
    )(*args)
    return (res[0], res[1]) if nxt is not None else (res[0], None)


def _w_spec(w, lead, k, tn, col_block0):
    none = (None,) * len(lead)
    return pl.BlockSpec(none + (k, tn), lambda i, j: lead + (0, j + col_block0))


def _dual_specs(xp, xs, tm, tn, nj, single_buffer_x=False):
    k = xp.shape[1]
    x_kw = dict(pipeline_mode=pl.Buffered(1)) if single_buffer_x else {}
    ms = xs.shape[0]
    in_specs = [pl.BlockSpec((tm, k), lambda i, j: (i, 0), **x_kw), pl.BlockSpec((ms, k), lambda i, j: (0, 0))]
    out_specs = [pl.BlockSpec((tm, tn), lambda i, j: (i, j)),
                 pl.BlockSpec((ms, tn), lambda i, j: (0, jnp.where(i == 0, j, nj - 1)))]
    return in_specs, out_specs


def _first_sweep(fn):
    pl.when(pl.program_id(0) == 0)(fn)


def _mm_kernel(xp_ref, xs_ref, w_ref, op_ref, os_ref):
    w = w_ref[...]
    op_ref[...] = _dot(xp_ref[...], w).astype(op_ref.dtype)

    def sample():
        os_ref[...] = _dot(xs_ref[...], w).astype(os_ref.dtype)
    _first_sweep(sample)


def _mm(x2, w, lead, col0, n, tm, tn, out_dtype, name, single_buffer_x=False):
    xp, xs = x2
    (m, k), ms = xp.shape, xs.shape[0]
    nj = n // tn
    in_specs, out_specs = _dual_specs(xp, xs, tm, tn, nj, single_buffer_x)
    x_bytes = tm * k * xp.dtype.itemsize * (1 if single_buffer_x else 2) + 2 * ms * k * 4
    vmem = x_bytes + 2 * k * tn * 4 + 2 * (tm + ms) * tn * 4
    return pl.pallas_call(
        _mm_kernel,
        out_shape=[jax.ShapeDtypeStruct((m, n), out_dtype), jax.ShapeDtypeStruct((ms, n), out_dtype)],
        grid=(m // tm, nj),
        in_specs=in_specs + [_w_spec(w, lead, k, tn, col0 // tn)],
        out_specs=out_specs,
        compiler_params=_cparams(2, vmem),
        name=name,
    )(xp, xs, w)


def _swiglu_kernel(hp_ref, hs_ref, wg_ref, wu_ref, ap_ref, as_ref):
    wg, wu = wg_ref[...], wu_ref[...]

    def swiglu(h_ref, a_ref):
        h = h_ref[...]
        g = _dot(h, wg)
        a_ref[...] = (g * _sigmoid(g) * _dot(h, wu)).astype(a_ref.dtype)
    swiglu(hp_ref, ap_ref)
    _first_sweep(functools.partial(swiglu, hs_ref, as_ref))


def _mm_swiglu(h2, w_in, lead, tm):
    hp, hs = h2
    (m, k), ms = hp.shape, hs.shape[0]
    tn = FF_TILE
    nj = D_FF // tn
    in_specs, out_specs = _dual_specs(hp, hs, tm, tn, nj)
    vmem = 2 * tm * k * 2 + 2 * ms * k * 4 + 4 * k * tn * 4 + 2 * (tm + ms) * tn * 2 + 3 * tm * tn * 4
    return pl.pallas_call(
        _swiglu_kernel,
        out_shape=[jax.ShapeDtypeStruct((m, D_FF), BF16), jax.ShapeDtypeStruct((ms, D_FF), BF16)],
        grid=(m // tm, nj),
        in_specs=in_specs + [_w_spec(w_in, lead, k, tn, 0), _w_spec(w_in, lead, k, tn, nj)],
        out_specs=out_specs,
        compiler_params=_cparams(2, vmem),
        name="ffn_in",
    )(hp, hs, w_in, w_in)


def _conv_in_kernel(hp_ref, hs_ref, wb_ref, wc_ref, wx_ref, bp_ref, up_ref, bs_ref, us_ref):
    wb, wc, wx = wb_ref[...], wc_ref[...], wx_ref[...]

    def gates(h_ref, b_ref, u_ref):
        h = h_ref[...]
        b_ref[...] = _dot(h, wb)
        u_ref[...] = _dot(h, wc) * _dot(h, wx)
    gates(hp_ref, bp_ref, up_ref)
    _first_sweep(functools.partial(gates, hs_ref, bs_ref, us_ref))


def _mm_conv_in(h2, w_in, lead, tm):
    hp, hs = h2
    (m, k), ms = hp.shape, hs.shape[0]
    tn = 256
    nb = D_MODEL // tn
    in_specs, (op_spec, os_spec) = _dual_specs(hp, hs, tm, tn, nb)
    vmem = 2 * tm * k * 2 + 2 * ms * k * 4 + 6 * k * tn * 4 + 4 * (tm + ms) * tn * 4 + 3 * tm * tn * 4
    out_p = jax.ShapeDtypeStruct((m, D_MODEL), F32)
    out_s = jax.ShapeDtypeStruct((ms, D_MODEL), F32)
    bp, up, bs, us = pl.pallas_call(
        _conv_in_kernel,
        out_shape=[out_p, out_p, out_s, out_s],
        grid=(m // tm, nb),
        in_specs=in_specs + [_w_spec(w_in, lead, k, tn, 0), _w_spec(w_in, lead, k, tn, nb), _w_spec(w_in, lead, k, tn, 2 * nb)],
        out_specs=[op_spec, op_spec, os_spec, os_spec],
        compiler_params=_cparams(2, vmem),
        name="conv_in",
    )(hp, hs, w_in, w_in, w_in)
    return (bp, bs), (up, us)


def _conv_kernel(u_ref, halo_ref, b_ref, cw_ref, z_ref, *, blocks_per_seq, zero_first):
    u = u_ref[...]
    halo = halo_ref[...]
    if zero_first:
        first = (pl.program_id(0) % blocks_per_seq) == 0
        halo = jnp.where(first, 0.0, halo)
    row = lax.broadcasted_iota(jnp.int32, u.shape, 0)
    prev1 = halo[7:8, :]
    prev2 = halo[6:7, :]
    u1 = jnp.where(row == 0, prev1, pltpu.roll(u, 1, 0))
    u2 = jnp.where(row == 0, prev2, jnp.where(row == 1, prev1, pltpu.roll(u, 2, 0)))
    conv = cw_ref[0:1, :] * u2 + cw_ref[1:2, :] * u1 + cw_ref[2:3, :] * u
    z_ref[...] = (b_ref[...] * conv).astype(z_ref.dtype)


def _conv(geo, u, halo_src, b, conv_w, zero_first):
    r = geo.rows
    tn = 1024
    if zero_first:
        halo_map = lambda i, j: (jnp.maximum(i * (r // 8) - 1, 0), j)
    else:
        halo_map = lambda i, j: (i, j)
    return pl.pallas_call(
        functools.partial(_conv_kernel, blocks_per_seq=geo.blocks_per_seq, zero_first=zero_first),
        out_shape=jax.ShapeDtypeStruct((geo.m, D_MODEL), geo.act),
        grid=(geo.m // r, D_MODEL // tn),
        in_specs=[pl.BlockSpec((r, tn), lambda i, j: (i, j)), pl.BlockSpec((8, tn), halo_map),
                  pl.BlockSpec((r, tn), lambda i, j: (i, j)), pl.BlockSpec((CONV_W, tn), lambda i, j: (0, j))],
        out_specs=pl.BlockSpec((r, tn), lambda i, j: (i, j)),
        compiler_params=_cparams(2, 2 * r * tn * 24),
        name="conv",
    )(u, halo_src, b, conv_w)


def _lambda(lq1_ref, lk1_ref, lq2_ref, lk2_ref, lam_init):
    a = jnp.sum(lq1_ref[...] * lk1_ref[...], axis=-1, keepdims=True)
    b = jnp.sum(lq2_ref[...] * lk2_ref[...], axis=-1, keepdims=True)
    return jnp.exp(a) - jnp.exp(b) + lam_init


def _attn_prompt_tile(qi, lam, sub_ref, q_ref, k_ref, v_ref, o_ref, *, tq, nq, lam_init):
    r = lax.broadcasted_iota(jnp.int32, (tq, tq), 0)
    c = lax.broadcasted_iota(jnp.int32, (tq, tq), 1)
    causal = c <= r

    def tile(n0):
        q = q_ref[...]
        outs = []
        for j in range(2):
            cols = slice(j * HEAD_DIM, (j + 1) * HEAD_DIM)
            qj = q[:, cols]
            s_d = jnp.where(causal, _dot_nt(qj, k_ref[n0:n0 + tq, cols]) * SCALE, NEG_INF)
            m = jnp.max(s_d, axis=-1, keepdims=True)
            if n0:
                s_p = _dot_nt(qj, k_ref[0:n0, cols]) * SCALE
                m = jnp.maximum(m, jnp.max(s_p, axis=-1, keepdims=True))
            p_d = jnp.exp(s_d - m)
            l = jnp.sum(p_d, axis=-1, keepdims=True)
            acc = _dot(p_d, v_ref[n0:n0 + tq, :])
            if n0:
                p_p = jnp.exp(s_p - m)
                l = l + jnp.sum(p_p, axis=-1, keepdims=True)
                acc = acc + _dot(p_p, v_ref[0:n0, :])
            outs.append(acc / l)
        o = outs[0] - lam * outs[1]
        o = _rms(o, SUBLN_EPS) * sub_ref[...] * (1.0 - lam_init)
        o_ref[...] = o.astype(o_ref.dtype)

    for i in range(nq):
        pl.when(qi == i)(functools.partial(tile, i * tq))


SUBLANES = 8
HJ_GROUPS = 2 * N_HEADS // SUBLANES


def _gather_rows(refs, i):
    parts = [ref.reshape(PAGE_SIZE * SUBLANES, HEAD_DIM)[pl.ds(i, PAGE_SIZE, stride=SUBLANES), :] for ref in refs]
    return parts[0] if len(parts) == 1 else jnp.concatenate(parts, axis=0)


def _attn_sample_step(p, lam_refs, sub_ref, q_ref, kn_ref, vn_ref, rest, *, n_steps, pps, lam_init):
    n_kv = HJ_GROUPS * pps
    k_refs = [rest[s * HJ_GROUPS:(s + 1) * HJ_GROUPS] for s in range(pps)]
    v_refs = [rest[n_kv + s * 4:n_kv + (s + 1) * 4] for s in range(pps)]
    o_ref, qbd_ref, m_ref, l_ref, acc_ref = rest[n_kv + 4 * pps:]
    row = lax.broadcasted_iota(jnp.int32, (SUBLANES, V_DIM), 0)
    col = lax.broadcasted_iota(jnp.int32, (SUBLANES, V_DIM), 1)

    @pl.when(p == 0)
    def _():
        for h in range(N_HEADS):
            qh = q_ref[:, h * V_DIM:(h + 1) * V_DIM]
            lower = pltpu.roll(qh, SAMPLE_LEN, 0)
            top = (row < SAMPLE_LEN) & (col < HEAD_DIM)
            bot = (row >= SAMPLE_LEN) & (col >= HEAD_DIM)
            qbd_ref[h] = jnp.where(top, qh, jnp.where(bot, lower, 0.0))
        m_ref[...] = jnp.full(m_ref.shape, NEG_INF, F32)
        l_ref[...] = jnp.zeros(l_ref.shape, F32)
        acc_ref[...] = jnp.zeros(acc_ref.shape, F32)

    scores = []
    for h in range(N_HEADS):
        g, i0 = (2 * h) // SUBLANES, (2 * h) % SUBLANES
        k0 = _gather_rows([k_refs[s][g] for s in range(pps)], i0)
        k1 = _gather_rows([k_refs[s][g] for s in range(pps)], i0 + 1)
        scores.append(_dot_nt(qbd_ref[h], jnp.concatenate([k0, k1], axis=1)))
    s = jnp.concatenate(scores, axis=0) * SCALE
    m_old = m_ref[...]
    m_new = jnp.maximum(m_old, jnp.max(s, axis=-1, keepdims=True))
    corr = jnp.exp(m_old - m_new)
    pr = jnp.exp(s - m_new)
    l_ref[...] = l_ref[...] * corr + jnp.sum(pr, axis=-1, keepdims=True)
    m_ref[...] = m_new
    pv = []
    for h in range(N_HEADS):
        hg, hi = h // SUBLANES, h % SUBLANES
        v = jnp.concatenate([_gather_rows([v_refs[s][2 * hg + half] for s in range(pps)], hi) for half in range(2)], axis=1)
        pv.append(_dot(pr[h * SUBLANES:(h + 1) * SUBLANES], v))
    acc_ref[...] = acc_ref[...] * corr + jnp.concatenate(pv, axis=0)

    @pl.when(p == n_steps - 1)
    def _():
        lam = _lambda(*lam_refs, lam_init)
        t_q = row[:, 0:1] % SAMPLE_LEN
        for h in range(N_HEADS):
            cols = slice(h * V_DIM, (h + 1) * V_DIM)
            qbd = qbd_ref[h]
            kn = kn_ref[:, cols]
            vn = vn_ref[:, cols]
            s_new = [jnp.where(t <= t_q, jnp.sum(qbd * kn[t:t + 1, :], axis=-1, keepdims=True) * SCALE, NEG_INF)
                     for t in range(SAMPLE_LEN)]
            rows = slice(h * SUBLANES, (h + 1) * SUBLANES)
            m_old = m_ref[rows, :]
            m_new = m_old
            for s_t in s_new:
                m_new = jnp.maximum(m_new, s_t)
            corr = jnp.exp(m_old - m_new)
            l = l_ref[rows, :] * corr
            acc = acc_ref[rows, :] * corr
            for t, s_t in enumerate(s_new):
                p_t = jnp.exp(s_t - m_new)
                l = l + p_t
                acc = acc + p_t * vn[t:t + 1, :]
            o = acc / l
            d = o - lam * pltpu.roll(o, SAMPLE_LEN, 0)
            o_ref[:, cols] = _rms(d, SUBLN_EPS) * sub_ref[...] * (1.0 - lam_init)


def _attn_kernel(pt_ref, lq1_ref, lk1_ref, lq2_ref, lk2_ref, sub_ref, qp_ref, kp_ref, vp_ref, qs_ref, kn_ref, vn_ref, *rest,
                 tq, nq, n_steps, pps, lam_init):
    n_cache = (HJ_GROUPS + 4) * pps
    op_ref = rest[n_cache]
    sample_rest = rest[:n_cache] + rest[n_cache + 1:]
    lam_refs = (lq1_ref, lk1_ref, lq2_ref, lk2_ref)
    step = (pl.program_id(0) * N_HEADS + pl.program_id(1)) * nq + pl.program_id(2)
    _attn_sample_step(step % n_steps, lam_refs, sub_ref, qs_ref, kn_ref, vn_ref, sample_rest,
                      n_steps=n_steps, pps=pps, lam_init=lam_init)
    _attn_prompt_tile(pl.program_id(2), _lambda(*lam_refs, lam_init), sub_ref, qp_ref, kp_ref, vp_ref, op_ref,
                      tq=tq, nq=nq, lam_init=lam_init)


def _attn(qkv_p, qkv_s, cache_k3, cache_v3, page0, page_table, lam_vecs, subln, lam_init, n_seq_p, seq_len):
    tq = 256
    nq = seq_len // tq
    n_seq_s, n_pages = page_table.shape
    n_steps_total = n_seq_p * N_HEADS * nq
    n_steps = n_steps_total // n_seq_s
    pps = n_pages // n_steps
    assert n_steps * n_seq_s == n_steps_total and pps * n_steps == n_pages

    def lin(b, h, i):
        return (b * N_HEADS + h) * nq + i

    vec = pl.BlockSpec((1, HEAD_DIM), lambda b, h, i, pt: (0, 0))
    kv_spec = pl.BlockSpec((seq_len, V_DIM), lambda b, h, i, pt: (b, h))
    q_spec = pl.BlockSpec((tq, V_DIM), lambda b, h, i, pt: (b * nq + i, h))
    row_spec = pl.BlockSpec((SAMPLE_PAD, D_MODEL), lambda b, h, i, pt: (lin(b, h, i) // n_steps, 0))
    tile_block = (PAGE_SIZE, SUBLANES, HEAD_DIM)

    def page(b, h, i, pt, s):
        n = lin(b, h, i)
        return page0 + pt[n // n_steps, (n % n_steps) * pps + s]

    k_specs = [pl.BlockSpec(tile_block, (lambda b, h, i, pt, s=s, g=g: (page(b, h, i, pt, s), g, 0)))
               for s in range(pps) for g in range(HJ_GROUPS)]
    v_specs = [pl.BlockSpec(tile_block, (lambda b, h, i, pt, s=s, g=g, c=c: (page(b, h, i, pt, s), g, c)))
               for s in range(pps) for g in range(2) for c in range(2)]
    grid_spec = pltpu.PrefetchScalarGridSpec(
        num_scalar_prefetch=1,
        grid=(n_seq_p, N_HEADS, nq),
        in_specs=[vec, vec, vec, vec, pl.BlockSpec((1, V_DIM), lambda b, h, i, pt: (0, 0)),
                  q_spec, kv_spec, kv_spec, row_spec, row_spec, row_spec] + k_specs + v_specs,
        out_specs=[q_spec, row_spec],
        scratch_shapes=[pltpu.VMEM((N_HEADS, SUBLANES, V_DIM), F32), pltpu.VMEM((N_HEADS * SUBLANES, 1), F32),
                        pltpu.VMEM((N_HEADS * SUBLANES, 1), F32), pltpu.VMEM((N_HEADS * SUBLANES, V_DIM), F32)],
    )
    n_blocks = (HJ_GROUPS + 4) * pps
    vmem = 2 * n_blocks * PAGE_SIZE * SUBLANES * HEAD_DIM * 4 + 4 * seq_len * V_DIM * 4 + 10 * tq * seq_len * 4
    return pl.pallas_call(
        functools.partial(_attn_kernel, tq=tq, nq=nq, n_steps=n_steps, pps=pps, lam_init=lam_init),
        out_shape=[jax.ShapeDtypeStruct((n_seq_p * seq_len, N_HEADS * V_DIM), BF16),
                   jax.ShapeDtypeStruct((n_seq_s * SAMPLE_PAD, D_MODEL), F32)],
        grid_spec=grid_spec,
        compiler_params=_cparams(3, vmem),
        name="attn",
    )(page_table, *lam_vecs, subln, *qkv_p, *qkv_s, *([cache_k3] * (HJ_GROUPS * pps)), *([cache_v3] * (4 * pps)))


def _run_layers(geos, tm, x2, mods, p, mixer_conv, mixer_attn):
    norm_pre, norm_post = p["norm_pre"], p["norm_post"]
    h2 = [_prenorm(geo, x, mods[0], norm_pre, 0, 0) for geo, x in zip(geos, x2)]
    aux = {}
    for l in range(DEPTH):
        for sub in range(N_SUB):
            g_row = l * N_SUB + sub
            if sub == 1:
                if l % 2 == 0:
                    y2, aux[l] = mixer_conv(l // 2, h2)
                else:
                    y2, aux[l] = mixer_attn(l // 2, l, h2)
                coef = 1.0
            else:
                a2 = _mm_swiglu(h2, p["ffn_w_in"], (l, sub // 2), tm)
                y2 = _mm(a2, p["ffn_w_out"], (l, sub // 2), 0, D_MODEL, tm, 256, F32, "ffn_out", single_buffer_x=True)
                coef = 0.5
            if sub + 1 < N_SUB:
                nxt, mod_next = (sub + 1, g_row + 1), mods[l]
            elif l + 1 < DEPTH:
                nxt, mod_next = (0, g_row + 1), mods[l + 1]
            else:
                nxt, mod_next = None, None
            res = [_post(geo, x, y, mods[l], norm_post, sub, g_row, coef, nxt, mod_next, norm_pre)
                   for geo, x, y in zip(geos, x2, y2)]
            x2 = [r[0] for r in res]
            h2 = [r[1] for r in res]
    return x2, aux


def kernel(x_prompt, x_sample, state_conv, cache_k, cache_v, page_table, c_prompt, c_sample,
           ada_w, ada_b, norm_pre, norm_post, ffn_w_in, ffn_w_out,
           conv_w_in, conv_w, conv_w_out, attn_w_qkv, attn_w_out,
           lambda_q1, lambda_k1, lambda_q2, lambda_k2, subln_w):
    p = dict(norm_pre=norm_pre.reshape(DEPTH * N_SUB, D_MODEL), norm_post=norm_post.reshape(DEPTH * N_SUB, D_MODEL),
             ffn_w_in=ffn_w_in, ffn_w_out=ffn_w_out)
    c_all = jnp.zeros((N_SEQ_ALL, D_MODEL), F32).at[:N_PROMPT_SEQ].set(c_prompt)
    c_all = c_all.at[N_PROMPT_SEQ:N_PROMPT_SEQ + N_SAMPLE_SEQ].set(c_sample)
    ada_b3 = ada_b.reshape(DEPTH, 1, N_SUB * N_MOD * D_MODEL)
    mods = [_ada(c_all, ada_w, ada_b3, l) for l in range(DEPTH)]

    m_p = N_PROMPT_SEQ * PROMPT_LEN
    m_s = N_SAMPLE_SEQ * SAMPLE_PAD
    geo_p = _Rows(m_p, 256, PROMPT_LEN, 0, BF16)
    geo_s = _Rows(m_s, SAMPLE_PAD, SAMPLE_PAD, N_PROMPT_SEQ, F32)
    geos = (geo_p, geo_s)
    tm = 1024
    n_kv = cache_k.shape[1]
    cache_k3 = cache_k.reshape(-1, 2 * N_HEADS, HEAD_DIM)
    cache_v3 = cache_v.reshape(-1, N_HEADS, V_DIM)

    def lam_vecs(i):
        return [a[i].reshape(1, HEAD_DIM) for a in (lambda_q1, lambda_k1, lambda_q2, lambda_k2)]

    def conv_mixer(i, h2):
        b2, u2 = _mm_conv_in(h2, conv_w_in, (i,), tm)
        up, us = u2
        state_p = up.reshape(N_PROMPT_SEQ, PROMPT_LEN, D_MODEL)[:, PROMPT_LEN - (CONV_W - 1):]
        state_s = us.reshape(N_SAMPLE_SEQ, SAMPLE_PAD, D_MODEL)[:, SAMPLE_LEN - (CONV_W - 1):SAMPLE_LEN]
        halo_s = jnp.pad(state_conv[i], ((0, 0), (SAMPLE_PAD - (CONV_W - 1), 0), (0, 0))).reshape(m_s, D_MODEL)
        z2 = (_conv(geo_p, up, up, b2[0], conv_w[i], zero_first=True),
              _conv(geo_s, us, halo_s, b2[1], conv_w[i], zero_first=False))
        y2 = _mm(z2, conv_w_out, (i,), 0, D_MODEL, tm, 512, F32, "conv_out")
        return y2, (state_p, state_s)

    def attn_mixer(i, l, h2):
        lam_init = 0.8 - 0.6 * math.exp(-0.3 * l)
        qp, qs = _mm(h2, attn_w_qkv, (i,), 0, D_MODEL, tm, 512, F32, "attn_q")
        kp, ks = _mm(h2, attn_w_qkv, (i,), D_MODEL, D_MODEL, tm, 512, F32, "attn_k")
        vp, vs = _mm(h2, attn_w_qkv, (i,), 2 * D_MODEL, D_MODEL, tm, 512, F32, "attn_v")
        sub = subln_w[i].reshape(1, V_DIM)
        o2 = _attn((qp, kp, vp), (qs, ks, vs), cache_k3, cache_v3, i * n_kv, page_table, lam_vecs(i), sub, lam_init,
                   N_PROMPT_SEQ, PROMPT_LEN)
        kv_p = (kp.reshape(N_PROMPT_SEQ, PROMPT_LEN, N_HEADS, 2, HEAD_DIM), vp.reshape(N_PROMPT_SEQ, PROMPT_LEN, N_HEADS, V_DIM))
        kv_s = (ks.reshape(N_SAMPLE_SEQ, SAMPLE_PAD, N_HEADS, 2, HEAD_DIM)[:, :SAMPLE_LEN],
                vs.reshape(N_SAMPLE_SEQ, SAMPLE_PAD, N_HEADS, V_DIM)[:, :SAMPLE_LEN])
        y2 = _mm(o2, attn_w_out, (i,), 0, D_MODEL, tm, 512, F32, "attn_out")
        return y2, (kv_p, kv_s)

    xp = x_prompt.reshape(m_p, D_MODEL)
    xs = jnp.pad(x_sample, ((0, 0), (0, SAMPLE_PAD - SAMPLE_LEN), (0, 0))).reshape(m_s, D_MODEL)
    (yp, ys), aux = _run_layers(geos, tm, (xp, xs), mods, p, conv_mixer, attn_mixer)

    conv_layers = [l for l in range(DEPTH) if l % 2 == 0]
    attn_layers = [l for l in range(DEPTH) if l % 2 == 1]
    y_prompt = yp.reshape(N_PROMPT_SEQ, PROMPT_LEN, D_MODEL)
    y_sample = ys.reshape(N_SAMPLE_SEQ, SAMPLE_PAD, D_MODEL)[:, :SAMPLE_LEN]
    return (y_prompt, y_sample,
            jnp.stack([aux[l][0] for l in conv_layers]), jnp.stack([aux[l][1] for l in conv_layers]),
            jnp.stack([aux[l][0][0] for l in attn_layers]), jnp.stack([aux[l][0][1] for l in attn_layers]),
            jnp.stack([aux[l][1][0] for l in attn_layers]), jnp.stack([aux[l][1][1] for l in attn_layers]))
```

```python
import functools
import math

import jax
import jax.numpy as jnp
from jax import lax
from jax.experimental import pallas as pl
from jax.experimental.pallas import tpu as pltpu

F32 = jnp.float32
BF16 = jnp.bfloat16

D_MODEL = 4096
N_PROMPT_SEQ = 4
PROMPT_LEN = 2048
DEPTH = 2
N_SAMPLE_SEQ = 8
SAMPLE_LEN = 4
SAMPLE_PAD = 8
PAGE_SIZE = 128
N_PAGES = 16384 // PAGE_SIZE
D_FF = 11008
CONV_W = 3
N_HEADS = 16
HEAD_DIM = 128
V_DIM = 2 * HEAD_DIM
SCALE = HEAD_DIM ** -0.5
RMS_EPS = 1e-6
SUBLN_EPS = 1e-5
NEG_INF = -1e30
N_SUB = 3
N_MOD = 3
N_SEQ_ALL = 16

V7X_VMEM_BYTES = 64 * 2**20
VMEM_CAP = V7X_VMEM_BYTES - 6 * 2**20
FF_TILE = 256
FFN_IN_TM = 2048


def _cparams(n_grid, vmem_bytes):
    limit = int(min(VMEM_CAP, max(32 * 2**20, vmem_bytes + 8 * 2**20)))
    return pltpu.CompilerParams(dimension_semantics=("arbitrary",) * n_grid, vmem_limit_bytes=limit)


def _dot(a, b):
    return lax.dot_general(a, b, (((1,), (0,)), ((), ())), preferred_element_type=F32)


def _dot_nt(a, b):
    return lax.dot_general(a, b, (((1,), (1,)), ((), ())), preferred_element_type=F32)


def _sigmoid(x):
    return 1.0 / (1.0 + jnp.exp(-x))


def _rms(x, eps):
    return x * lax.rsqrt(jnp.mean(x * x, axis=-1, keepdims=True) + eps)


def _ada_kernel(c_ref, w_ref, b_ref, o_ref):
    c = c_ref[...]
    o_ref[...] = _dot(c * _sigmoid(c), w_ref[...]) + b_ref[...]


def _ada(c_all, ada_w, ada_b3, layer):
    n = N_SUB * N_MOD * D_MODEL
    tn = 512
    out = pl.pallas_call(
        _ada_kernel,
        out_shape=jax.ShapeDtypeStruct((N_SEQ_ALL, n), F32),
        grid=(n // tn,),
        in_specs=[
            pl.BlockSpec((N_SEQ_ALL, D_MODEL), lambda j: (0, 0)),
            pl.BlockSpec((None, D_MODEL, tn), lambda j: (layer, 0, j)),
            pl.BlockSpec((None, 1, tn), lambda j: (layer, 0, j)),
        ],
        out_specs=pl.BlockSpec((N_SEQ_ALL, tn), lambda j: (0, j)),
        compiler_params=_cparams(1, 2 * D_MODEL * tn * 4),
        name=f"ada_l{layer}",
    )(c_all, ada_w, ada_b3)
    return out.reshape(N_SEQ_ALL, N_SUB * N_MOD, D_MODEL)


class _Rows:
    def __init__(self, m, rows, rows_per_seq, seq0, act):
        self.m, self.rows, self.blocks_per_seq, self.seq0 = m, rows, rows_per_seq // rows, seq0
        self.act = act

    def seq(self, i):
        return i // self.blocks_per_seq + self.seq0


def _modulate(x, mod_ref, sub, g_ref, g_row):
    shift = mod_ref[N_MOD * sub:N_MOD * sub + 1, :]
    scale = mod_ref[N_MOD * sub + 1:N_MOD * sub + 2, :]
    return _rms(x, RMS_EPS) * g_ref[g_row:g_row + 1, :] * (1.0 + scale) + shift


def _prenorm_kernel(x_ref, mod_ref, g_ref, h_ref, *, sub, g_row):
    h_ref[...] = _modulate(x_ref[...], mod_ref, sub, g_ref, g_row).astype(h_ref.dtype)


def _prenorm(geo, x, mod, g_pre, sub, g_row):
    r = geo.rows
    return pl.pallas_call(
        functools.partial(_prenorm_kernel, sub=sub, g_row=g_row),
        out_shape=jax.ShapeDtypeStruct((geo.m, D_MODEL), geo.act),
        grid=(geo.m // r,),
        in_specs=[
            pl.BlockSpec((r, D_MODEL), lambda i: (i, 0)),
            pl.BlockSpec((None, N_SUB * N_MOD, D_MODEL), lambda i: (geo.seq(i), 0, 0)),
            pl.BlockSpec(g_pre.shape, lambda i: (0, 0)),
        ],
        out_specs=pl.BlockSpec((r, D_MODEL), lambda i: (i, 0)),
        compiler_params=_cparams(1, 2 * r * D_MODEL * 6),
        name="prenorm",
    )(x, mod, g_pre)


def _post_kernel(x_ref, y_ref, mod_ref, gpost_ref, *rest, sub, g_row, coef, nxt):
    gate = mod_ref[N_MOD * sub + 2:N_MOD * sub + 3, :]
    y = _rms(y_ref[...], RMS_EPS) * gpost_ref[g_row:g_row + 1, :]
    x_new = x_ref[...] + (coef * gate) * y
    if nxt is None:
        (xo_ref,) = rest
        xo_ref[...] = x_new
    else:
        modn_ref, gpre_ref, xo_ref, h_ref = rest
        xo_ref[...] = x_new
        h_ref[...] = _modulate(x_new, modn_ref, nxt[0], gpre_ref, nxt[1]).astype(h_ref.dtype)


def _post(geo, x, y, mod, g_post, sub, g_row, coef, nxt=None, mod_next=None, g_pre=None):
    r = geo.rows
    row_spec = pl.BlockSpec((r, D_MODEL), lambda i: (i, 0))
    mod_spec = pl.BlockSpec((None, N_SUB * N_MOD, D_MODEL), lambda i: (geo.seq(i), 0, 0))
    in_specs = [row_spec, row_spec, mod_spec, pl.BlockSpec(g_post.shape, lambda i: (0, 0))]
    args = [x, y, mod, g_post]
    out_shape = [jax.ShapeDtypeStruct((geo.m, D_MODEL), F32)]
    out_specs = [row_spec]
    if nxt is not None:
        in_specs += [mod_spec, pl.BlockSpec(g_pre.shape, lambda i: (0, 0))]
        args += [mod_next, g_pre]
        out_shape.append(jax.ShapeDtypeStruct((geo.m, D_MODEL), geo.act))
        out_specs.append(row_spec)
    res = pl.pallas_call(
        functools.partial(_post_kernel, sub=sub, g_row=g_row, coef=coef, nxt=nxt),
        out_shape=out_shape,
        grid=(geo.m // r,),
        in_specs=in_specs,
        out_specs=out_specs,
        compiler_params=_cparams(1, 2 * r * D_MODEL * 14),
        name="post",
    )(*args)
    return (res[0], res[1]) if nxt is not None else (res[0], None)


def _w_spec(w, lead, k, tn, col_block0):
    none = (None,) * len(lead)
    return pl.BlockSpec(none + (k, tn), lambda i, j: lead + (0, j + col_block0))


def _dual_specs(xp, xs, tm, tn, nj, single_buffer_x=False):
    k = xp.shape[1]
    x_kw = dict(pipeline_mode=pl.Buffered(1)) if single_buffer_x else {}
    ms = xs.shape[0]
    in_specs = [pl.BlockSpec((tm, k), lambda i, j: (i, 0), **x_kw), pl.BlockSpec((ms, k), lambda i, j: (0, 0))]
    out_specs = [pl.BlockSpec((tm, tn), lambda i, j: (i, j)),
                 pl.BlockSpec((ms, tn), lambda i, j: (0, jnp.where(i == 0, j, nj - 1)))]
    return in_specs, out_specs


def _first_sweep(fn):
    pl.when(pl.program_id(0) == 0)(fn)


def _mm_kernel(xp_ref, xs_ref, w_ref, op_ref, os_ref):
    w = w_ref[...]
    op_ref[...] = _dot(xp_ref[...], w).astype(op_ref.dtype)

    def sample():
        os_ref[...] = _dot(xs_ref[...], w).astype(os_ref.dtype)
    _first_sweep(sample)


def _mm(x2, w, lead, col0, n, tm, tn, out_dtype, name, single_buffer_x=False):
    xp, xs = x2
    (m, k), ms = xp.shape, xs.shape[0]
    nj = n // tn
    in_specs, out_specs = _dual_specs(xp, xs, tm, tn, nj, single_buffer_x)
    x_bytes = tm * k * xp.dtype.itemsize * (1 if single_buffer_x else 2) + 2 * ms * k * 4
    vmem = x_bytes + 2 * k * tn * 4 + 2 * (tm + ms) * tn * 4
    return pl.pallas_call(
        _mm_kernel,
        out_shape=[jax.ShapeDtypeStruct((m, n), out_dtype), jax.ShapeDtypeStruct((ms, n), out_dtype)],
        grid=(m // tm, nj),
        in_specs=in_specs + [_w_spec(w, lead, k, tn, col0 // tn)],
        out_specs=out_specs,
        compiler_params=_cparams(2, vmem),
        name=name,
    )(xp, xs, w)


def _swiglu_kernel(hp_ref, hs_ref, wg_ref, wu_ref, ap_ref, as_ref):
    wg, wu = wg_ref[...], wu_ref[...]

    def swiglu(h_ref, a_ref):
        h = h_ref[...]
        g = _dot(h, wg)
        a_ref[...] = (g * _sigmoid(g) * _dot(h, wu)).astype(a_ref.dtype)
    swiglu(hp_ref, ap_ref)
    _first_sweep(functools.partial(swiglu, hs_ref, as_ref))


def _mm_swiglu(h2, w_in, lead, tm):
    hp, hs = h2
    (m, k), ms = hp.shape, hs.shape[0]
    tn = FF_TILE
    nj = D_FF // tn
    in_specs, out_specs = _dual_specs(hp, hs, tm, tn, nj, single_buffer_x=True)
    vmem = tm * k * 2 + 2 * ms * k * 4 + 4 * k * tn * 4 + 2 * (tm + ms) * tn * 2 + 3 * tm * tn * 4
    return pl.pallas_call(
        _swiglu_kernel,
        out_shape=[jax.ShapeDtypeStruct((m, D_FF), BF16), jax.ShapeDtypeStruct((ms, D_FF), BF16)],
        grid=(m // tm, nj),
        in_specs=in_specs + [_w_spec(w_in, lead, k, tn, 0), _w_spec(w_in, lead, k, tn, nj)],
        out_specs=out_specs,
        compiler_params=_cparams(2, vmem),
        name="ffn_in",
    )(hp, hs, w_in, w_in)


def _conv_in_kernel(hp_ref, hs_ref, wb_ref, wc_ref, wx_ref, bp_ref, up_ref, bs_ref, us_ref):
    wb, wc, wx = wb_ref[...], wc_ref[...], wx_ref[...]

    def gates(h_ref, b_ref, u_ref):
        h = h_ref[...]
        b_ref[...] = _dot(h, wb)
        u_ref[...] = _dot(h, wc) * _dot(h, wx)
    gates(hp_ref, bp_ref, up_ref)
    _first_sweep(functools.partial(gates, hs_ref, bs_ref, us_ref))


def _mm_conv_in(h2, w_in, lead, tm):
    hp, hs = h2
    (m, k), ms = hp.shape, hs.shape[0]
    tn = 256
    nb = D_MODEL // tn
    in_specs, (op_spec, os_spec) = _dual_specs(hp, hs, tm, tn, nb)
    vmem = 2 * tm * k * 2 + 2 * ms * k * 4 + 6 * k * tn * 4 + 4 * (tm + ms) * tn * 4 + 3 * tm * tn * 4
    out_p = jax.ShapeDtypeStruct((m, D_MODEL), F32)
    out_s = jax.ShapeDtypeStruct((ms, D_MODEL), F32)
    bp, up, bs, us = pl.pallas_call(
        _conv_in_kernel,
        out_shape=[out_p, out_p, out_s, out_s],
        grid=(m // tm, nb),
        in_specs=in_specs + [_w_spec(w_in, lead, k, tn, 0), _w_spec(w_in, lead, k, tn, nb), _w_spec(w_in, lead, k, tn, 2 * nb)],
        out_specs=[op_spec, op_spec, os_spec, os_spec],
        compiler_params=_cparams(2, vmem),
        name="conv_in",
    )(hp, hs, w_in, w_in, w_in)
    return (bp, bs), (up, us)


def _conv_kernel(u_ref, halo_ref, b_ref, cw_ref, z_ref, *, blocks_per_seq, zero_first):
    u = u_ref[...]
    halo = halo_ref[...]
    if zero_first:
        first = (pl.program_id(0) % blocks_per_seq) == 0
        halo = jnp.where(first, 0.0, halo)
    row = lax.broadcasted_iota(jnp.int32, u.shape, 0)
    prev1 = halo[7:8, :]
    prev2 = halo[6:7, :]
    u1 = jnp.where(row == 0, prev1, pltpu.roll(u, 1, 0))
    u2 = jnp.where(row == 0, prev2, jnp.where(row == 1, prev1, pltpu.roll(u, 2, 0)))
    conv = cw_ref[0:1, :] * u2 + cw_ref[1:2, :] * u1 + cw_ref[2:3, :] * u
    z_ref[...] = (b_ref[...] * conv).astype(z_ref.dtype)


def _conv(geo, u, halo_src, b, conv_w, zero_first):
    r = geo.rows
    tn = 1024
    if zero_first:
        halo_map = lambda i, j: (jnp.maximum(i * (r // 8) - 1, 0), j)
    else:
        halo_map = lambda i, j: (i, j)
    return pl.pallas_call(
        functools.partial(_conv_kernel, blocks_per_seq=geo.blocks_per_seq, zero_first=zero_first),
        out_shape=jax.ShapeDtypeStruct((geo.m, D_MODEL), geo.act),
        grid=(geo.m // r, D_MODEL // tn),
        in_specs=[pl.BlockSpec((r, tn), lambda i, j: (i, j)), pl.BlockSpec((8, tn), halo_map),
                  pl.BlockSpec((r, tn), lambda i, j: (i, j)), pl.BlockSpec((CONV_W, tn), lambda i, j: (0, j))],
        out_specs=pl.BlockSpec((r, tn), lambda i, j: (i, j)),
        compiler_params=_cparams(2, 2 * r * tn * 24),
        name="conv",
    )(u, halo_src, b, conv_w)


def _lambda(lq1_ref, lk1_ref, lq2_ref, lk2_ref, lam_init):
    a = jnp.sum(lq1_ref[...] * lk1_ref[...], axis=-1, keepdims=True)
    b = jnp.sum(lq2_ref[...] * lk2_ref[...], axis=-1, keepdims=True)
    return jnp.exp(a) - jnp.exp(b) + lam_init


def _attn_prompt_tile(qi, lam, sub_ref, q_ref, k_ref, v_ref, o_ref, *, tq, nq, lam_init):
    r = lax.broadcasted_iota(jnp.int32, (tq, tq), 0)
    c = lax.broadcasted_iota(jnp.int32, (tq, tq), 1)
    causal = c <= r

    def tile(n0):
        q = q_ref[...]
        outs = []
        for j in range(2):
            cols = slice(j * HEAD_DIM, (j + 1) * HEAD_DIM)
            qj = q[:, cols]
            s_d = jnp.where(causal, _dot_nt(qj, k_ref[n0:n0 + tq, cols]) * SCALE, NEG_INF)
            m = jnp.max(s_d, axis=-1, keepdims=True)
            if n0:
                s_p = _dot_nt(qj, k_ref[0:n0, cols]) * SCALE
                m = jnp.maximum(m, jnp.max(s_p, axis=-1, keepdims=True))
            p_d = jnp.exp(s_d - m)
            l = jnp.sum(p_d, axis=-1, keepdims=True)
            acc = _dot(p_d, v_ref[n0:n0 + tq, :])
            if n0:
                p_p = jnp.exp(s_p - m)
                l = l + jnp.sum(p_p, axis=-1, keepdims=True)
                acc = acc + _dot(p_p, v_ref[0:n0, :])
            outs.append(acc / l)
        o = outs[0] - lam * outs[1]
        o = _rms(o, SUBLN_EPS) * sub_ref[...] * (1.0 - lam_init)
        o_ref[...] = o.astype(o_ref.dtype)

    for i in range(nq):
        pl.when(qi == i)(functools.partial(tile, i * tq))


SUBLANES = 8
HJ_GROUPS = 2 * N_HEADS // SUBLANES


def _gather_rows(refs, i):
    parts = [ref.reshape(PAGE_SIZE * SUBLANES, HEAD_DIM)[pl.ds(i, PAGE_SIZE, stride=SUBLANES), :] for ref in refs]
    return parts[0] if len(parts) == 1 else jnp.concatenate(parts, axis=0)


def _attn_sample_step(p, lam_refs, sub_ref, q_ref, kn_ref, vn_ref, rest, *, n_steps, pps, lam_init):
    n_kv = HJ_GROUPS * pps
    k_refs = [rest[s * HJ_GROUPS:(s + 1) * HJ_GROUPS] for s in range(pps)]
    v_refs = [rest[n_kv + s * 4:n_kv + (s + 1) * 4] for s in range(pps)]
    o_ref, qbd_ref, m_ref, l_ref, acc_ref = rest[n_kv + 4 * pps:]
    row = lax.broadcasted_iota(jnp.int32, (SUBLANES, V_DIM), 0)
    col = lax.broadcasted_iota(jnp.int32, (SUBLANES, V_DIM), 1)

    @pl.when(p == 0)
    def _():
        for h in range(N_HEADS):
            qh = q_ref[:, h * V_DIM:(h + 1) * V_DIM]
            lower = pltpu.roll(qh, SAMPLE_LEN, 0)
            top = (row < SAMPLE_LEN) & (col < HEAD_DIM)
            bot = (row >= SAMPLE_LEN) & (col >= HEAD_DIM)
            qbd_ref[h] = jnp.where(top, qh, jnp.where(bot, lower, 0.0))
        m_ref[...] = jnp.full(m_ref.shape, NEG_INF, F32)
        l_ref[...] = jnp.zeros(l_ref.shape, F32)
        acc_ref[...] = jnp.zeros(acc_ref.shape, F32)

    scores = []
    for h in range(N_HEADS):
        g, i0 = (2 * h) // SUBLANES, (2 * h) % SUBLANES
        k0 = _gather_rows([k_refs[s][g] for s in range(pps)], i0)
        k1 = _gather_rows([k_refs[s][g] for s in range(pps)], i0 + 1)
        scores.append(_dot_nt(qbd_ref[h], jnp.concatenate([k0, k1], axis=1)))
    s = jnp.concatenate(scores, axis=0) * SCALE
    m_old = m_ref[...]
    m_new = jnp.maximum(m_old, jnp.max(s, axis=-1, keepdims=True))
    corr = jnp.exp(m_old - m_new)
    pr = jnp.exp(s - m_new)
    l_ref[...] = l_ref[...] * corr + jnp.sum(pr, axis=-1, keepdims=True)
    m_ref[...] = m_new
    pv = []
    for h in range(N_HEADS):
        hg, hi = h // SUBLANES, h % SUBLANES
        v = jnp.concatenate([_gather_rows([v_refs[s][2 * hg + half] for s in range(pps)], hi) for half in range(2)], axis=1)
        pv.append(_dot(pr[h * SUBLANES:(h + 1) * SUBLANES], v))
    acc_ref[...] = acc_ref[...] * corr + jnp.concatenate(pv, axis=0)

    @pl.when(p == n_steps - 1)
    def _():
        lam = _lambda(*lam_refs, lam_init)
        t_q = row[:, 0:1] % SAMPLE_LEN
        for h in range(N_HEADS):
            cols = slice(h * V_DIM, (h + 1) * V_DIM)
            qbd = qbd_ref[h]
            kn = kn_ref[:, cols]
            vn = vn_ref[:, cols]
            s_new = [jnp.where(t <= t_q, jnp.sum(qbd * kn[t:t + 1, :], axis=-1, keepdims=True) * SCALE, NEG_INF)
                     for t in range(SAMPLE_LEN)]
            rows = slice(h * SUBLANES, (h + 1) * SUBLANES)
            m_old = m_ref[rows, :]
            m_new = m_old
            for s_t in s_new:
                m_new = jnp.maximum(m_new, s_t)
            corr = jnp.exp(m_old - m_new)
            l = l_ref[rows, :] * corr
            acc = acc_ref[rows, :] * corr
            for t, s_t in enumerate(s_new):
                p_t = jnp.exp(s_t - m_new)
                l = l + p_t
                acc = acc + p_t * vn[t:t + 1, :]
            o = acc / l
            d = o - lam * pltpu.roll(o, SAMPLE_LEN, 0)
            o_ref[:, cols] = _rms(d, SUBLN_EPS) * sub_ref[...] * (1.0 - lam_init)


def _attn_kernel(pt_ref, lq1_ref, lk1_ref, lq2_ref, lk2_ref, sub_ref, qp_ref, kp_ref, vp_ref, qs_ref, kn_ref, vn_ref, *rest,
                 tq, nq, n_steps, pps, lam_init):
    n_cache = (HJ_GROUPS + 4) * pps
    op_ref = rest[n_cache]
    sample_rest = rest[:n_cache] + rest[n_cache + 1:]
    lam_refs = (lq1_ref, lk1_ref, lq2_ref, lk2_ref)
    step = (pl.program_id(0) * N_HEADS + pl.program_id(1)) * nq + pl.program_id(2)
    _attn_sample_step(step % n_steps, lam_refs, sub_ref, qs_ref, kn_ref, vn_ref, sample_rest,
                      n_steps=n_steps, pps=pps, lam_init=lam_init)
    _attn_prompt_tile(pl.program_id(2), _lambda(*lam_refs, lam_init), sub_ref, qp_ref, kp_ref, vp_ref, op_ref,
                      tq=tq, nq=nq, lam_init=lam_init)


def _attn(qkv_p, qkv_s, cache_k3, cache_v3, page0, page_table, lam_vecs, subln, lam_init, n_seq_p, seq_len):
    tq = 256
    nq = seq_len // tq
    n_seq_s, n_pages = page_table.shape
    n_steps_total = n_seq_p * N_HEADS * nq
    n_steps = n_steps_total // n_seq_s
    pps = n_pages // n_steps
    assert n_steps * n_seq_s == n_steps_total and pps * n_steps == n_pages

    def lin(b, h, i):
        return (b * N_HEADS + h) * nq + i

    vec = pl.BlockSpec((1, HEAD_DIM), lambda b, h, i, pt: (0, 0))
    kv_spec = pl.BlockSpec((seq_len, V_DIM), lambda b, h, i, pt: (b, h))
    q_spec = pl.BlockSpec((tq, V_DIM), lambda b, h, i, pt: (b * nq + i, h))
    row_spec = pl.BlockSpec((SAMPLE_PAD, D_MODEL), lambda b, h, i, pt: (lin(b, h, i) // n_steps, 0))
    tile_block = (PAGE_SIZE, SUBLANES, HEAD_DIM)

    def page(b, h, i, pt, s):
        n = lin(b, h, i)
        return page0 + pt[n // n_steps, (n % n_steps) * pps + s]

    k_specs = [pl.BlockSpec(tile_block, (lambda b, h, i, pt, s=s, g=g: (page(b, h, i, pt, s), g, 0)))
               for s in range(pps) for g in range(HJ_GROUPS)]
    v_specs = [pl.BlockSpec(tile_block, (lambda b, h, i, pt, s=s, g=g, c=c: (page(b, h, i, pt, s), g, c)))
               for s in range(pps) for g in range(2) for c in range(2)]
    grid_spec = pltpu.PrefetchScalarGridSpec(
        num_scalar_prefetch=1,
        grid=(n_seq_p, N_HEADS, nq),
        in_specs=[vec, vec, vec, vec, pl.BlockSpec((1, V_DIM), lambda b, h, i, pt: (0, 0)),
                  q_spec, kv_spec, kv_spec, row_spec, row_spec, row_spec] + k_specs + v_specs,
        out_specs=[q_spec, row_spec],
        scratch_shapes=[pltpu.VMEM((N_HEADS, SUBLANES, V_DIM), F32), pltpu.VMEM((N_HEADS * SUBLANES, 1), F32),
                        pltpu.VMEM((N_HEADS * SUBLANES, 1), F32), pltpu.VMEM((N_HEADS * SUBLANES, V_DIM), F32)],
    )
    n_blocks = (HJ_GROUPS + 4) * pps
    vmem = 2 * n_blocks * PAGE_SIZE * SUBLANES * HEAD_DIM * 4 + 4 * seq_len * V_DIM * 4 + 10 * tq * seq_len * 4
    return pl.pallas_call(
        functools.partial(_attn_kernel, tq=tq, nq=nq, n_steps=n_steps, pps=pps, lam_init=lam_init),
        out_shape=[jax.ShapeDtypeStruct((n_seq_p * seq_len, N_HEADS * V_DIM), BF16),
                   jax.ShapeDtypeStruct((n_seq_s * SAMPLE_PAD, D_MODEL), F32)],
        grid_spec=grid_spec,
        compiler_params=_cparams(3, vmem),
        name="attn",
    )(page_table, *lam_vecs, subln, *qkv_p, *qkv_s, *([cache_k3] * (HJ_GROUPS * pps)), *([cache_v3] * (4 * pps)))


def _run_layers(geos, tm, x2, mods, p, mixer_conv, mixer_attn):
    norm_pre, norm_post = p["norm_pre"], p["norm_post"]
    h2 = [_prenorm(geo, x, mods[0], norm_pre, 0, 0) for geo, x in zip(geos, x2)]
    aux = {}
    for l in range(DEPTH):
        for sub in range(N_SUB):
            g_row = l * N_SUB + sub
            if sub == 1:
                if l % 2 == 0:
                    y2, aux[l] = mixer_conv(l // 2, h2)
                else:
                    y2, aux[l] = mixer_attn(l // 2, l, h2)
                coef = 1.0
            else:
                a2 = _mm_swiglu(h2, p["ffn_w_in"], (l, sub // 2), FFN_IN_TM)
                y2 = _mm(a2, p["ffn_w_out"], (l, sub // 2), 0, D_MODEL, tm, 256, F32, "ffn_out", single_buffer_x=True)
                coef = 0.5
            if sub + 1 < N_SUB:
                nxt, mod_next = (sub + 1, g_row + 1), mods[l]
            elif l + 1 < DEPTH:
                nxt, mod_next = (0, g_row + 1), mods[l + 1]
            else:
                nxt, mod_next = None, None
            res = [_post(geo, x, y, mods[l], norm_post, sub, g_row, coef, nxt, mod_next, norm_pre)
                   for geo, x, y in zip(geos, x2, y2)]
            x2 = [r[0] for r in res]
            h2 = [r[1] for r in res]
    return x2, aux


def kernel(x_prompt, x_sample, state_conv, cache_k, cache_v, page_table, c_prompt, c_sample,
           ada_w, ada_b, norm_pre, norm_post, ffn_w_in, ffn_w_out,
           conv_w_in, conv_w, conv_w_out, attn_w_qkv, attn_w_out,
           lambda_q1, lambda_k1, lambda_q2, lambda_k2, subln_w):
    p = dict(norm_pre=norm_pre.reshape(DEPTH * N_SUB, D_MODEL), norm_post=norm_post.reshape(DEPTH * N_SUB, D_MODEL),
             ffn_w_in=ffn_w_in, ffn_w_out=ffn_w_out)
    c_all = jnp.zeros((N_SEQ_ALL, D_MODEL), F32).at[:N_PROMPT_SEQ].set(c_prompt)
    c_all = c_all.at[N_PROMPT_SEQ:N_PROMPT_SEQ + N_SAMPLE_SEQ].set(c_sample)
    ada_b3 = ada_b.reshape(DEPTH, 1, N_SUB * N_MOD * D_MODEL)
    mods = [_ada(c_all, ada_w, ada_b3, l) for l in range(DEPTH)]

    m_p = N_PROMPT_SEQ * PROMPT_LEN
    m_s = N_SAMPLE_SEQ * SAMPLE_PAD
    geo_p = _Rows(m_p, 256, PROMPT_LEN, 0, BF16)
    geo_s = _Rows(m_s, SAMPLE_PAD, SAMPLE_PAD, N_PROMPT_SEQ, F32)
    geos = (geo_p, geo_s)
    tm = 1024
    n_kv = cache_k.shape[1]
    cache_k3 = cache_k.reshape(-1, 2 * N_HEADS, HEAD_DIM)
    cache_v3 = cache_v.reshape(-1, N_HEADS, V_DIM)

    def lam_vecs(i):
        return [a[i].reshape(1, HEAD_DIM) for a in (lambda_q1, lambda_k1, lambda_q2, lambda_k2)]

    def conv_mixer(i, h2):
        b2, u2 = _mm_conv_in(h2, conv_w_in, (i,), tm)
        up, us = u2
        state_p = up.reshape(N_PROMPT_SEQ, PROMPT_LEN, D_MODEL)[:, PROMPT_LEN - (CONV_W - 1):]
        state_s = us.reshape(N_SAMPLE_SEQ, SAMPLE_PAD, D_MODEL)[:, SAMPLE_LEN - (CONV_W - 1):SAMPLE_LEN]
        halo_s = jnp.pad(state_conv[i], ((0, 0), (SAMPLE_PAD - (CONV_W - 1), 0), (0, 0))).reshape(m_s, D_MODEL)
        z2 = (_conv(geo_p, up, up, b2[0], conv_w[i], zero_first=True),
              _conv(geo_s, us, halo_s, b2[1], conv_w[i], zero_first=False))
        y2 = _mm(z2, conv_w_out, (i,), 0, D_MODEL, tm, 512, F32, "conv_out")
        return y2, (state_p, state_s)

    def attn_mixer(i, l, h2):
        lam_init = 0.8 - 0.6 * math.exp(-0.3 * l)
        qp, qs = _mm(h2, attn_w_qkv, (i,), 0, D_MODEL, tm, 512, F32, "attn_q")
        kp, ks = _mm(h2, attn_w_qkv, (i,), D_MODEL, D_MODEL, tm, 512, F32, "attn_k")
        vp, vs = _mm(h2, attn_w_qkv, (i,), 2 * D_MODEL, D_MODEL, tm, 512, F32, "attn_v")
        sub = subln_w[i].reshape(1, V_DIM)
        o2 = _attn((qp, kp, vp), (qs, ks, vs), cache_k3, cache_v3, i * n_kv, page_table, lam_vecs(i), sub, lam_init,
                   N_PROMPT_SEQ, PROMPT_LEN)
        kv_p = (kp.reshape(N_PROMPT_SEQ, PROMPT_LEN, N_HEADS, 2, HEAD_DIM), vp.reshape(N_PROMPT_SEQ, PROMPT_LEN, N_HEADS, V_DIM))
        kv_s = (ks.reshape(N_SAMPLE_SEQ, SAMPLE_PAD, N_HEADS, 2, HEAD_DIM)[:, :SAMPLE_LEN],
                vs.reshape(N_SAMPLE_SEQ, SAMPLE_PAD, N_HEADS, V_DIM)[:, :SAMPLE_LEN])
        y2 = _mm(o2, attn_w_out, (i,), 0, D_MODEL, tm, 512, F32, "attn_out")
        return y2, (kv_p, kv_s)

    xp = x_prompt.reshape(m_p, D_MODEL)
    xs = jnp.pad(x_sample, ((0, 0), (0, SAMPLE_PAD - SAMPLE_LEN), (0, 0))).reshape(m_s, D_MODEL)
    (yp, ys), aux = _run_layers(geos, tm, (xp, xs), mods, p, conv_mixer, attn_mixer)

    conv_layers = [l for l in range(DEPTH) if l % 2 == 0]
    attn_layers = [l for l in range(DEPTH) if l % 2 == 1]
    y_prompt = yp.reshape(N_PROMPT_SEQ, PROMPT_LEN, D_MODEL)
    y_sample = ys.reshape(N_SAMPLE_SEQ, SAMPLE_PAD, D_MODEL)[:, :SAMPLE_LEN]
    return (y_prompt, y_sample,
            jnp.stack([aux[l][0] for l in conv_layers]), jnp.stack([aux[l][1] for l in conv_layers]),
            jnp.stack([aux[l][0][0] for l in attn_layers]), jnp.stack([aux[l][0][1] for l in attn_layers]),
            jnp.stack([aux[l][1][0] for l in attn_layers]), jnp.stack([aux[l][1][1] for l in attn_layers]))
```
